```python
import jax, jax.numpy as jnp
from jax import lax
import numpy as np

D_MODEL = 2048
BATCH = 4
SEQ = 4096
DEPTH = 2

CHUNK = 64
Q_BLOCK = 128
EPS = 1e-6

SSM_D_INNER = 2 * D_MODEL
SSM_HEAD_DIM = 64
SSM_HEADS = SSM_D_INNER // SSM_HEAD_DIM
SSM_GROUPS = 8
SSM_HEADS_PER_GROUP = SSM_HEADS // SSM_GROUPS
SSM_STATE = 128
SSM_CONV = 4
SSM_CONV_DIM = SSM_D_INNER + 2 * SSM_GROUPS * SSM_STATE

MLA_HEADS = 16
MLA_Q_RANK = 512
MLA_KV_RANK = 512
MLA_NOPE = 128
MLA_ROPE = 64
MLA_V = 128
MLA_QK = MLA_NOPE + MLA_ROPE
MLA_OUT = MLA_HEADS * MLA_V
ROPE_BASE = 10000.0

FFN_DIM = 5632
FFN_CONV = 3

IN_SIZES = (SSM_D_INNER,
            SSM_CONV_DIM,
            SSM_HEADS,
            MLA_Q_RANK,
            MLA_KV_RANK,
            MLA_ROPE,
            D_MODEL,
            D_MODEL)
IN_DIM = sum(IN_SIZES)

kernel_name = 'hybrid_ssd_mla_gated_block'


def rms_norm(x, w):
    xf = x.astype(jnp.float32)
    y = xf * lax.rsqrt(jnp.mean(xf * xf, axis=-1, keepdims=True) + EPS)
    return (y * w.astype(jnp.float32)).astype(x.dtype)


def modulate(h, shift, scale):
    return h * (1.0 + scale[:, None, :]) + shift[:, None, :]


def split_cols(t, sizes):
    idx, acc = [], 0
    for s in sizes[:-1]:
        acc += s
        idx.append(acc)
    return jnp.split(t, idx, axis=-1)


def causal_dwconv(x, w, b):
    k = w.shape[0]
    y = lax.conv_general_dilated(x, w[:, None, :].astype(x.dtype), window_strides=(1,),
                                 padding=[(k - 1, 0)], dimension_numbers=('NWC', 'WIO', 'NWC'),
                                 feature_group_count=x.shape[-1])
    return y + b.astype(x.dtype)


def apply_rope(x, cos, sin):
    half = x.shape[-1] // 2
    x1, x2 = x[..., :half], x[..., half:]
    cos = cos.astype(x.dtype)
    sin = sin.astype(x.dtype)
    return jnp.concatenate([x1 * cos - x2 * sin, x2 * cos + x1 * sin], axis=-1)


def ssd_chunked_scan(xs, dt, a, bm, cm):
    bsz, seq = xs.shape[0], xs.shape[1]
    nc = seq // CHUNK

    def to_chunks(t):
        return jnp.moveaxis(t.reshape((bsz, nc, CHUNK) + t.shape[2:]), 1, 0)

    causal = jnp.tril(jnp.ones((CHUNK, CHUNK), dtype=bool))[None, :, :, None, None]

    def step(state, inp):
        x_c, dt_c, b_c, c_c = inp
        acs = jnp.cumsum(dt_c * a, axis=1)
        seg = acs[:, :, None] - acs[:, None, :]
        lmask = jnp.exp(jnp.where(causal, seg, -jnp.inf))
        cb = jnp.einsum('bqgn,bkgn->bqkg', c_c, b_c)
        y_diag = jnp.einsum('bqkg,bqkgr,bkgr,bkgrp->bqgrp', cb, lmask, dt_c, x_c)
        y_off = jnp.einsum('bqgn,bgrpn->bqgrp', c_c, state) * jnp.exp(acs)[..., None]
        decay = jnp.exp(acs[:, -1:] - acs) * dt_c
        new_state = (state * jnp.exp(acs[:, -1])[..., None, None]
                     + jnp.einsum('bkgn,bkgr,bkgrp->bgrpn', b_c, decay, x_c))
        return new_state, y_diag + y_off

    state0 = jnp.zeros((bsz, SSM_GROUPS, SSM_HEADS_PER_GROUP, SSM_HEAD_DIM, SSM_STATE), jnp.float32)
    _, ys = lax.scan(step, state0, (to_chunks(xs), to_chunks(dt), to_chunks(bm), to_chunks(cm)))
    return jnp.moveaxis(ys, 0, 1).reshape(xs.shape)


def ssd_branch(z, xbc, dt_raw, conv_w, conv_b, dt_bias, a_log, d_skip, norm_w, w_out):
    bsz, seq, _ = z.shape
    f32 = jnp.float32
    xbc = jax.nn.silu(causal_dwconv(xbc, conv_w, conv_b))
    xs, bm, cm = split_cols(xbc, (SSM_D_INNER, SSM_GROUPS * SSM_STATE, SSM_GROUPS * SSM_STATE))
    xs = xs.astype(f32).reshape(bsz, seq, SSM_GROUPS, SSM_HEADS_PER_GROUP, SSM_HEAD_DIM)
    bm = bm.astype(f32).reshape(bsz, seq, SSM_GROUPS, SSM_STATE)
    cm = cm.astype(f32).reshape(bsz, seq, SSM_GROUPS, SSM_STATE)
    dt = jax.nn.softplus(dt_raw.astype(f32) + dt_bias.astype(f32))
    dt = dt.reshape(bsz, seq, SSM_GROUPS, SSM_HEADS_PER_GROUP)
    a = -jnp.exp(a_log.astype(f32)).reshape(SSM_GROUPS, SSM_HEADS_PER_GROUP)
    y = ssd_chunked_scan(xs, dt, a, bm, cm)
    y = y + xs * d_skip.astype(f32).reshape(SSM_GROUPS, SSM_HEADS_PER_GROUP)[..., None]
    y = y.reshape(bsz, seq, SSM_D_INNER).astype(z.dtype)
    y = rms_norm(y * jax.nn.silu(z), norm_w)
    return y @ w_out


def chunk_causal_attention(q, k, v):
    bsz, seq, nh, dk = q.shape
    nb = seq // Q_BLOCK
    qb = jnp.moveaxis(q.reshape(bsz, nb, Q_BLOCK, nh, dk), 1, 0)
    key_chunk = jnp.arange(seq) // CHUNK
    scale = dk ** -0.5

    def one_block(args):
        qi, i = args
        s = jnp.einsum('bqhd,bkhd->bhqk', qi, k).astype(jnp.float32) * scale
        q_chunk = (i * Q_BLOCK + jnp.arange(Q_BLOCK)) // CHUNK
        mask = key_chunk[None, :] <= q_chunk[:, None]
        p = jax.nn.softmax(jnp.where(mask, s, -jnp.inf), axis=-1).astype(v.dtype)
        return jnp.einsum('bhqk,bkhd->bqhd', p, v)

    out = lax.map(one_block, (qb, jnp.arange(nb)))
    return jnp.moveaxis(out, 0, 1).reshape(bsz, seq, nh, v.shape[-1])


def mla_branch(q_lat, kv_lat, k_rope, cos, sin, q_norm_w, w_q_up, kv_norm_w, w_kv_up,
               qn_w, kn_w, w_out):
    bsz, seq, _ = q_lat.shape
    q = (rms_norm(q_lat, q_norm_w) @ w_q_up).reshape(bsz, seq, MLA_HEADS, MLA_QK)
    kv = (rms_norm(kv_lat, kv_norm_w) @ w_kv_up).reshape(bsz, seq, MLA_HEADS, MLA_NOPE + MLA_V)
    k_nope, v = kv[..., :MLA_NOPE], kv[..., MLA_NOPE:]
    k = jnp.concatenate([k_nope, jnp.broadcast_to(k_rope[:, :, None, :],
                                                  (bsz, seq, MLA_HEADS, MLA_ROPE))], axis=-1)
    q = rms_norm(q, qn_w)
    k = rms_norm(k, kn_w)
    q = jnp.concatenate([q[..., :MLA_NOPE], apply_rope(q[..., MLA_NOPE:], cos, sin)], axis=-1)
    k = jnp.concatenate([k[..., :MLA_NOPE], apply_rope(k[..., MLA_NOPE:], cos, sin)], axis=-1)
    o = chunk_causal_attention(q, k, v)
    return o.reshape(bsz, seq, MLA_OUT) @ w_out


def conv_glu(h, w_up, conv_w, conv_b, w_down):
    u = h @ w_up
    gate, val = u[..., :FFN_DIM], u[..., FFN_DIM:]
    gate = causal_dwconv(gate, conv_w, conv_b)
    return (jax.nn.silu(gate) * val) @ w_down


def setup_inputs(seed: int = 0) -> dict:
    key = jax.random.key(seed)
    ks = jax.random.split(key, 32)
    f32 = jnp.float32

    def nrm(k, shape, scale):
        return jax.random.normal(k, shape, f32) * scale

    def gain(k, shape):
        return 1.0 + 0.02 * jax.random.normal(k, shape, f32)

    L = DEPTH
    x = jax.random.normal(ks[0], (BATCH, SEQ, D_MODEL), f32)
    c = jax.random.normal(ks[1], (BATCH, D_MODEL), f32)
    offset = jax.random.randint(ks[2], (BATCH, 1), 0, 64, dtype=jnp.int32) * CHUNK
    positions = offset + jnp.arange(SEQ, dtype=jnp.int32)[None, :]
    dt_init = jnp.exp(jax.random.uniform(ks[3], (L, SSM_HEADS), f32, np.log(1e-3), np.log(1e-1)))
    ssm_dt_bias = dt_init + jnp.log(-jnp.expm1(-dt_init))
    ssm_a_log = jnp.log(jax.random.uniform(ks[4], (L, SSM_HEADS), f32, 1.0, 16.0))
    return {
        'x': x,
        'c': c,
        'positions': positions,
        'norm1_w': gain(ks[5], (L, D_MODEL)),
        'norm2_w': gain(ks[6], (L, D_MODEL)),
        'w_mod': nrm(ks[7], (L, D_MODEL, 6 * D_MODEL), 0.5 * D_MODEL ** -0.5),
        'b_mod': nrm(ks[8], (L, 6 * D_MODEL), 0.02),
        'w_in': nrm(ks[9], (L, D_MODEL, IN_DIM), D_MODEL ** -0.5),
        'ssm_conv_w': nrm(ks[10], (L, SSM_CONV, SSM_CONV_DIM), SSM_CONV ** -0.5),
        'ssm_conv_b': nrm(ks[11], (L, SSM_CONV_DIM), 0.02),
        'ssm_dt_bias': ssm_dt_bias,
        'ssm_a_log': ssm_a_log,
        'ssm_d': gain(ks[12], (L, SSM_HEADS)),
        'ssm_norm_w': gain(ks[13], (L, SSM_D_INNER)),
        'w_ssm_out': nrm(ks[14], (L, SSM_D_INNER, D_MODEL), SSM_D_INNER ** -0.5),
        'mla_q_norm_w': gain(ks[15], (L, MLA_Q_RANK)),
        'w_q_up': nrm(ks[16], (L, MLA_Q_RANK, MLA_HEADS * MLA_QK), MLA_Q_RANK ** -0.5),
        'mla_kv_norm_w': gain(ks[17], (L, MLA_KV_RANK)),
        'w_kv_up': nrm(ks[18], (L, MLA_KV_RANK, MLA_HEADS * (MLA_NOPE + MLA_V)), MLA_KV_RANK ** -0.5),
        'qk_norm_q_w': gain(ks[19], (L, MLA_QK)),
        'qk_norm_k_w': gain(ks[20], (L, MLA_QK)),
        'w_mla_out': nrm(ks[21], (L, MLA_OUT, D_MODEL), MLA_OUT ** -0.5),
        'w_mix_out': nrm(ks[22], (L, D_MODEL, D_MODEL), D_MODEL ** -0.5),
        'w_ffn_up': nrm(ks[23], (L, D_MODEL, 2 * FFN_DIM), D_MODEL ** -0.5),
        'ffn_conv_w': nrm(ks[24], (L, FFN_CONV, FFN_DIM), FFN_CONV ** -0.5),
        'ffn_conv_b': nrm(ks[25], (L, FFN_DIM), 0.02),
        'w_ffn_down': nrm(ks[26], (L, FFN_DIM, D_MODEL), FFN_DIM ** -0.5),
    }


def reference(x, c, positions, norm1_w, norm2_w, w_mod, b_mod, w_in, ssm_conv_w, ssm_conv_b,
              ssm_dt_bias, ssm_a_log, ssm_d, ssm_norm_w, w_ssm_out, mla_q_norm_w, w_q_up,
              mla_kv_norm_w, w_kv_up, qk_norm_q_w, qk_norm_k_w, w_mla_out, w_mix_out,
              w_ffn_up, ffn_conv_w, ffn_conv_b, w_ffn_down):
    inv_freq = ROPE_BASE ** (-jnp.arange(0, MLA_ROPE, 2, dtype=jnp.float32) / MLA_ROPE)
    ang = positions.astype(jnp.float32)[..., None] * inv_freq
    cos = jnp.cos(ang)[:, :, None, :]
    sin = jnp.sin(ang)[:, :, None, :]
    cond = jax.nn.silu(c)
    for l in range(DEPTH):
        mod = cond @ w_mod[l] + b_mod[l]
        shift1, scale1, gate1, shift2, scale2, gate2 = jnp.split(mod, 6, axis=-1)

        h = modulate(rms_norm(x, norm1_w[l]), shift1, scale1)
        z, xbc, dt_raw, q_lat, kv_lat, k_rope, ga, gb = split_cols(h @ w_in[l], IN_SIZES)
        y_ssd = ssd_branch(z, xbc, dt_raw, ssm_conv_w[l], ssm_conv_b[l], ssm_dt_bias[l],
                           ssm_a_log[l], ssm_d[l], ssm_norm_w[l], w_ssm_out[l])
        y_mla = mla_branch(q_lat, kv_lat, k_rope, cos, sin, mla_q_norm_w[l], w_q_up[l],
                           mla_kv_norm_w[l], w_kv_up[l], qk_norm_q_w[l], qk_norm_k_w[l],
                           w_mla_out[l])
        merged = jax.nn.sigmoid(ga) * y_ssd + jax.nn.sigmoid(gb) * y_mla
        x = x + gate1[:, None, :] * (merged @ w_mix_out[l])

        h = modulate(rms_norm(x, norm2_w[l]), shift2, scale2)
        x = x + gate2[:, None, :] * conv_glu(h, w_ffn_up[l], ffn_conv_w[l], ffn_conv_b[l],
                                             w_ffn_down[l])
    return x
```

```python
import functools
import math

import jax
import jax.numpy as jnp
import numpy as np
from jax import lax
from jax.experimental import pallas as pl
from jax.experimental.pallas import tpu as pltpu

F32 = jnp.float32
BF16 = jnp.bfloat16

D_MODEL = 2048
CHUNK = 64
EPS = 1e-6

SSM_D_INNER = 2 * D_MODEL
SSM_HEAD_DIM = 64
SSM_HEADS = SSM_D_INNER // SSM_HEAD_DIM
SSM_GROUPS = 8
SSM_HEADS_PER_GROUP = SSM_HEADS // SSM_GROUPS
SSM_STATE = 128
SSM_CONV = 4
SSM_BC = SSM_GROUPS * SSM_STATE
SSM_CONV_DIM = SSM_D_INNER + 2 * SSM_BC
SSM_GROUP_WIDTH = SSM_HEADS_PER_GROUP * SSM_HEAD_DIM

MLA_HEADS = 16
MLA_Q_RANK = 512
MLA_KV_RANK = 512
MLA_NOPE = 128
MLA_ROPE = 64
MLA_V = 128
MLA_QK = MLA_NOPE + MLA_ROPE
MLA_OUT = MLA_HEADS * MLA_V
MLA_QK_PAD = 256
ROPE_BASE = 10000.0
ATTN_SCALE = MLA_QK ** -0.5

FFN_DIM = 5632
FFN_CONV = 3

IN_SIZES = (SSM_D_INNER, SSM_CONV_DIM, SSM_HEADS, MLA_Q_RANK, MLA_KV_RANK, MLA_ROPE, D_MODEL, D_MODEL)
IN_OFFS = tuple(int(v) for v in np.cumsum((0,) + IN_SIZES))

MAIN_Z = 0
MAIN_XBC = SSM_D_INNER
MAIN_GA = MAIN_XBC + SSM_CONV_DIM
MAIN_GB = MAIN_GA + D_MODEL
MAIN_DIM = MAIN_GB + D_MODEL
SMALL_QLAT = 0
SMALL_KVLAT = MLA_Q_RANK
SMALL_ROPE = SMALL_KVLAT + MLA_KV_RANK
SMALL_DT = SMALL_ROPE + 2 * MLA_ROPE
SMALL_DIM = SMALL_DT + 128

LANES = 128
HALO = 8
VMEM_LIMIT_MB = 56


def _call(kernel, *, name, grid, in_specs, out_specs, out_shape, semantics, scratch=()):
    return pl.pallas_call(
        kernel, grid=grid, in_specs=in_specs, out_specs=out_specs, out_shape=out_shape,
        scratch_shapes=list(scratch), name=name,
        compiler_params=pltpu.CompilerParams(dimension_semantics=semantics,
                                             vmem_limit_bytes=VMEM_LIMIT_MB * 1024 * 1024))


def _silu(v):
    return v * jax.nn.sigmoid(v)


def _rms(v, w):
    ms = jnp.mean(v * v, axis=-1, keepdims=True)
    return v * lax.rsqrt(ms + EPS) * w


def _mod_kernel(c_ref, w_ref, b_ref, o_ref):
    cond = _silu(c_ref[...]).astype(BF16)
    o_ref[0] = jnp.dot(cond, w_ref[0].astype(BF16), preferred_element_type=F32) + b_ref[0]


def _modulation(c, w_mod, b_mod):
    depth, d, n = w_mod.shape
    bsz = c.shape[0]
    tn = 1024
    c8 = jnp.zeros((8, d), F32).at[:bsz].set(c)
    out = _call(
        _mod_kernel, name="adaln_mod", grid=(depth, n // tn),
        in_specs=[pl.BlockSpec((8, d), lambda l, j: (0, 0)),
                  pl.BlockSpec((1, d, tn), lambda l, j: (l, 0, j)),
                  pl.BlockSpec((1, 1, tn), lambda l, j: (l, 0, j))],
        out_specs=pl.BlockSpec((1, 8, tn), lambda l, j: (l, 0, j)),
        out_shape=jax.ShapeDtypeStruct((depth, 8, n), F32),
        semantics=("parallel", "parallel"))(c8, w_mod, b_mod.reshape(depth, 1, n))
    return out[:, :bsz]


def _rope_kernel(pos_ref, freq_ref, o_ref):
    ang = pos_ref[...].astype(F32) * freq_ref[...]
    lane = lax.broadcasted_iota(jnp.int32, ang.shape, 1)
    cos = jnp.cos(ang)
    sin = jnp.sin(ang)
    o_ref[...] = jnp.where(lane < 64, cos, jnp.where(lane < 96, -sin, sin))


def _rope_table(positions):
    t = positions.size
    rows = 2048
    half = MLA_ROPE // 2
    inv_freq = (ROPE_BASE ** (-np.arange(0, MLA_ROPE, 2, dtype=np.float32) / MLA_ROPE)).astype(np.float32)
    freq = jnp.asarray(np.tile(inv_freq, LANES // half)[None, :])
    return _call(
        _rope_kernel, name="rope_table", grid=(t // rows,),
        in_specs=[pl.BlockSpec((rows, 1), lambda i: (i, 0)),
                  pl.BlockSpec((1, LANES), lambda i: (0, 0))],
        out_specs=pl.BlockSpec((rows, LANES), lambda i: (i, 0)),
        out_shape=jax.ShapeDtypeStruct((t, LANES), F32),
        semantics=("parallel",))(positions.reshape(t, 1), freq)


def _norm_mm_kernel(x_ref, nw_ref, sh_ref, sc_ref, w_ref, o_ref, h_ref, *, tm, rows):
    @pl.when(pl.program_id(1) == 0)
    def _():
        nw = nw_ref[...]
        sc = 1.0 + sc_ref[0]
        sh = sh_ref[0]

        def body(r, carry):
            sl = pl.ds(pl.multiple_of(r * rows, rows), rows)
            h_ref[sl, :] = (_rms(x_ref[sl, :], nw) * sc + sh).astype(BF16)
            return carry

        lax.fori_loop(0, tm // rows, body, 0)

    o_ref[...] = jnp.dot(h_ref[...], w_ref[...], preferred_element_type=F32).astype(o_ref.dtype)


def _norm_matmul(x2, nw, shift, scale, w, *, seq, tm, tn, out_dtype, name):
    t, d = x2.shape
    n = w.shape[1]
    tpb = seq // tm
    kernel = functools.partial(_norm_mm_kernel, tm=tm, rows=256)
    return _call(
        kernel, name=name, grid=(t // tm, n // tn),
        in_specs=[pl.BlockSpec((tm, d), lambda i, j: (i, 0)),
                  pl.BlockSpec((1, d), lambda i, j: (0, 0)),
                  pl.BlockSpec((1, 1, d), lambda i, j: (i // tpb, 0, 0)),
                  pl.BlockSpec((1, 1, d), lambda i, j: (i // tpb, 0, 0)),
                  pl.BlockSpec((d, tn), lambda i, j: (0, j))],
        out_specs=pl.BlockSpec((tm, tn), lambda i, j: (i, j)),
        out_shape=jax.ShapeDtypeStruct((t, n), out_dtype),
        scratch=[pltpu.VMEM((tm, d), BF16)],
        semantics=("parallel", "arbitrary"))(x2, nw, shift, scale, w)


def _mm_resid_kernel(a_ref, w_ref, r_ref, g_ref, o_ref):
    acc = jnp.dot(a_ref[...], w_ref[...], preferred_element_type=F32)
    o_ref[...] = r_ref[...] + g_ref[0] * acc


def _matmul_residual(a, w, res, gate, *, seq, tm, tn, name):
    t, k = a.shape
    n = w.shape[1]
    tpb = seq // tm
    return _call(
        _mm_resid_kernel, name=name, grid=(t // tm, n // tn),
        in_specs=[pl.BlockSpec((tm, k), lambda i, j: (i, 0)),
                  pl.BlockSpec((k, tn), lambda i, j: (0, j)),
                  pl.BlockSpec((tm, tn), lambda i, j: (i, j)),
                  pl.BlockSpec((1, 1, tn), lambda i, j: (i // tpb, 0, j))],
        out_specs=pl.BlockSpec((tm, tn), lambda i, j: (i, j)),
        out_shape=jax.ShapeDtypeStruct((t, n), F32),
        semantics=("parallel", "parallel"))(a, w, res, gate)


def _merge_kernel(y_ref, o_ref, ws_ref, wm_ref, ga_ref, gb_ref, out_ref):
    ys = jnp.dot(y_ref[...], ws_ref[...], preferred_element_type=F32)
    ym = jnp.dot(o_ref[...], wm_ref[...], preferred_element_type=F32)
    ga = jax.nn.sigmoid(ga_ref[...].astype(F32))
    gb = jax.nn.sigmoid(gb_ref[...].astype(F32))
    out_ref[...] = (ga * ys + gb * ym).astype(out_ref.dtype)


def _merge(y_ssd, o_mla, w_ssm_out, w_mla_out, main, *, tm, tn):
    t = y_ssd.shape[0]
    n = w_ssm_out.shape[1]
    ga_blk = MAIN_GA // tn
    gb_blk = MAIN_GB // tn
    return _call(
        _merge_kernel, name="branch_merge", grid=(t // tm, n // tn),
        in_specs=[pl.BlockSpec((tm, y_ssd.shape[1]), lambda i, j: (i, 0)),
                  pl.BlockSpec((tm, o_mla.shape[1]), lambda i, j: (i, 0)),
                  pl.BlockSpec((w_ssm_out.shape[0], tn), lambda i, j: (0, j)),
                  pl.BlockSpec((w_mla_out.shape[0], tn), lambda i, j: (0, j)),
                  pl.BlockSpec((tm, tn), lambda i, j: (i, ga_blk + j)),
                  pl.BlockSpec((tm, tn), lambda i, j: (i, gb_blk + j))],
        out_specs=pl.BlockSpec((tm, tn), lambda i, j: (i, j)),
        out_shape=jax.ShapeDtypeStruct((t, n), BF16),
        semantics=("parallel", "parallel"))(y_ssd, o_mla, w_ssm_out, w_mla_out, main, main)


def _conv_kernel(x_ref, halo_ref, w_ref, b_ref, *rest, taps, ts, tiles_per_seq, with_val):
    if with_val:
        v_ref, o_ref, buf = rest
    else:
        o_ref, buf = rest
    at_seq_start = (pl.program_id(0) % tiles_per_seq) == 0
    buf[0:HALO, :] = jnp.where(at_seq_start, 0.0, halo_ref[...].astype(F32))
    buf[HALO:, :] = x_ref[...].astype(F32)
    w = w_ref[...]
    acc = b_ref[...] + w[taps - 1:taps, :] * buf[HALO:, :]
    for k in range(taps - 1):
        acc = acc + w[k:k + 1, :] * buf[pl.ds(HALO - (taps - 1) + k, ts), :]
    y = _silu(acc)
    if with_val:
        y = y * v_ref[...].astype(F32)
    o_ref[...] = y.astype(o_ref.dtype)


def _causal_conv_silu(src, w, b, *, seq, col0, width, val_col0, ts, tc, name):
    t = src.shape[0]
    taps = w.shape[0]
    with_val = val_col0 is not None
    cb = col0 // tc
    hb = ts // HALO
    kernel = functools.partial(_conv_kernel, taps=taps, ts=ts, tiles_per_seq=seq // ts, with_val=with_val)
    in_specs = [pl.BlockSpec((ts, tc), lambda i, j: (i, cb + j)),
                pl.BlockSpec((HALO, tc), lambda i, j: (jnp.maximum(i * hb - 1, 0), cb + j)),
                pl.BlockSpec((taps, tc), lambda i, j: (0, j)),
                pl.BlockSpec((1, tc), lambda i, j: (0, j))]
    args = [src, src, w, b.reshape(1, width)]
    if with_val:
        vb = val_col0 // tc
        in_specs.append(pl.BlockSpec((ts, tc), lambda i, j: (i, vb + j)))
        args.append(src)
    return _call(
        kernel, name=name, grid=(t // ts, width // tc),
        in_specs=in_specs,
        out_specs=pl.BlockSpec((ts, tc), lambda i, j: (i, j)),
        out_shape=jax.ShapeDtypeStruct((t, width), BF16),
        scratch=[pltpu.VMEM((ts + HALO, tc), F32)],
        semantics=("parallel", "parallel"))(*args)


def _split3(v):
    hi = v.astype(BF16).astype(F32)
    r1 = v - hi
    mid = r1.astype(BF16).astype(F32)
    lo = (r1 - mid).astype(BF16).astype(F32)
    return hi, mid, lo


def _pack3(v, lane):
    hi, mid, lo = _split3(v)
    first = lane < CHUNK
    p1 = jnp.where(first, hi, pltpu.roll(mid, CHUNK, axis=1))
    p2 = jnp.where(first, lo, 0.0)
    return jnp.concatenate([p1, p2], axis=1).astype(BF16)


def _ssd_kernel(xa_ref, b_ref, c_ref, z_ref, dt_ref, dtb_ref, alog_ref, dskip_ref, nw_ref, tri_ref, rep_ref,
                o_ref, state, y_scr):
    q = CHUNK
    gw = SSM_GROUP_WIDTH

    @pl.when(pl.program_id(1) == 0)
    def _():
        state[...] = jnp.zeros_like(state)

    lane = lax.broadcasted_iota(jnp.int32, (q, LANES), 1)
    x_dt = dt_ref[...] + dtb_ref[...]
    dt = jnp.maximum(x_dt, 0.0) + jnp.log1p(jnp.exp(-jnp.abs(x_dt)))
    dta = dt * (-jnp.exp(alog_ref[...]))

    hi, mid, lo = _split3(dta)
    stacked = jnp.concatenate([hi, mid, lo, jnp.zeros_like(hi)], axis=0).astype(BF16)
    acs = jnp.dot(tri_ref[...], stacked, preferred_element_type=F32)

    lhs = jnp.concatenate([_pack3(acs, lane), _pack3(dt, lane)], axis=0)

    row = lax.broadcasted_iota(jnp.int32, (q, gw), 0)
    col = lax.broadcasted_iota(jnp.int32, (q, gw), 1) % q
    causal = col <= row
    diag = col == row
    bd_r = lax.broadcasted_iota(jnp.int32, (4 * q, 4 * q), 0) // q
    bd_c = lax.broadcasted_iota(jnp.int32, (4 * q, 4 * q), 1) // q
    blockdiag = bd_r == bd_c

    for g in range(SSM_GROUPS):
        gs = slice(g * gw, (g + 1) * gw)
        ns = slice(g * SSM_STATE, (g + 1) * SSM_STATE)
        rg = jnp.dot(lhs, rep_ref[:, gs], preferred_element_type=F32)
        colb = rg[:q]
        dtrep = rg[q:]
        acs_row = jnp.sum(jnp.where(diag, colb, 0.0), axis=0, keepdims=True)
        dt_row = jnp.sum(jnp.where(diag, dtrep, 0.0), axis=0, keepdims=True)
        bg = b_ref[:, ns]
        cg = c_ref[:, ns]
        b_tiled = jnp.concatenate([bg] * SSM_HEADS_PER_GROUP, axis=0)
        cb = lax.dot_general(cg, b_tiled, (((1,), (1,)), ((), ())), preferred_element_type=F32)
        lmask = jnp.exp(jnp.where(causal, colb - acs_row, -jnp.inf))
        m = (cb * lmask * dt_row).astype(BF16)
        xg = xa_ref[:, gs]
        y_parts = []
        for quad in range(2):
            xs = xg[:, quad * 4 * q:(quad + 1) * 4 * q]
            x_bd = jnp.where(blockdiag, jnp.concatenate([xs] * 4, axis=0), jnp.zeros((), BF16))
            y_parts.append(jnp.dot(m[:, quad * 4 * q:(quad + 1) * 4 * q], x_bd, preferred_element_type=F32))
        y_diag = jnp.concatenate(y_parts, axis=1)
        st = state[g]
        y_off = jnp.dot(cg, st.astype(BF16), preferred_element_type=F32) * jnp.exp(colb)
        last = colb[q - 1:q, :]
        xf = xg.astype(F32)
        xd = (xf * (jnp.exp(last - colb) * dtrep)).astype(BF16)
        state[g] = st * jnp.exp(last) + lax.dot_general(bg, xd, (((0,), (0,)), ((), ())),
                                                        preferred_element_type=F32)
        y_scr[:, gs] = y_diag + y_off + xf * dskip_ref[:, gs]

    gated = y_scr[...] * _silu(z_ref[...].astype(F32))
    o_ref[...] = _rms(gated, nw_ref[...]).astype(o_ref.dtype)


def _ssd_constants():
    q = CHUNK
    tri = np.tril(np.ones((q, q), np.float32))
    tri4 = np.concatenate([tri, tri, tri, np.zeros_like(tri)], axis=1)
    rep = np.zeros((4 * q, SSM_D_INNER), np.float32)
    for j in range(3):
        for r in range(SSM_HEADS):
            rep[j * q + r, r * SSM_HEAD_DIM:(r + 1) * SSM_HEAD_DIM] = 1.0
    return jnp.asarray(tri4, BF16), jnp.asarray(rep, BF16)


def _ssd(xbc_act, main, small, dt_bias, a_log, d_skip, norm_w, *, bsz, seq):
    t = bsz * seq
    nc = seq // CHUNK
    q = CHUNK
    tri4, rep = _ssd_constants()
    pad = lambda v: jnp.concatenate([v.astype(F32), jnp.zeros((LANES - SSM_HEADS,), F32)]).reshape(1, LANES)
    dskip_row = jnp.repeat(d_skip.astype(F32), SSM_HEAD_DIM).reshape(1, SSM_D_INNER)
    rowmap = lambda b, c: b * nc + c
    return _call(
        _ssd_kernel, name="ssd_scan", grid=(bsz, nc),
        in_specs=[pl.BlockSpec((q, SSM_D_INNER), lambda b, c: (rowmap(b, c), 0)),
                  pl.BlockSpec((q, SSM_BC), lambda b, c: (rowmap(b, c), SSM_D_INNER // SSM_BC)),
                  pl.BlockSpec((q, SSM_BC), lambda b, c: (rowmap(b, c), SSM_D_INNER // SSM_BC + 1)),
                  pl.BlockSpec((q, SSM_D_INNER), lambda b, c: (rowmap(b, c), MAIN_Z // SSM_D_INNER)),
                  pl.BlockSpec((q, LANES), lambda b, c: (rowmap(b, c), SMALL_DT // LANES)),
                  pl.BlockSpec((1, LANES), lambda b, c: (0, 0)),
                  pl.BlockSpec((1, LANES), lambda b, c: (0, 0)),
                  pl.BlockSpec((1, SSM_D_INNER), lambda b, c: (0, 0)),
                  pl.BlockSpec((1, SSM_D_INNER), lambda b, c: (0, 0)),
                  pl.BlockSpec((q, 4 * q), lambda b, c: (0, 0)),
                  pl.BlockSpec((4 * q, SSM_D_INNER), lambda b, c: (0, 0))],
        out_specs=pl.BlockSpec((q, SSM_D_INNER), lambda b, c: (rowmap(b, c), 0)),
        out_shape=jax.ShapeDtypeStruct((t, SSM_D_INNER), BF16),
        scratch=[pltpu.VMEM((SSM_GROUPS, SSM_STATE, SSM_GROUP_WIDTH), F32),
                 pltpu.VMEM((q, SSM_D_INNER), F32)],
        semantics=("parallel", "arbitrary"))(
            xbc_act, xbc_act, xbc_act, main, small, pad(dt_bias), pad(a_log), dskip_row,
            norm_w.astype(F32).reshape(1, SSM_D_INNER), tri4, rep)


def _mla_prep_kernel(ql_ref, kvl_ref, rope_ref, tab_ref, qnw_ref, kvnw_ref, wq_ref, wkv_ref, gq_ref, gk_ref,
                     q_ref, k_ref, v_ref):
    qn = _rms(ql_ref[...], qnw_ref[...]).astype(BF16)
    kvn = _rms(kvl_ref[...], kvnw_ref[...]).astype(BF16)
    q = jnp.dot(qn, wq_ref[...], preferred_element_type=F32)
    kv = jnp.dot(kvn, wkv_ref[...], preferred_element_type=F32)
    tab = tab_ref[...]
    rope = rope_ref[...]
    lane = lax.broadcasted_iota(jnp.int32, rope.shape, 1)
    first = lane < MLA_ROPE
    gq = gq_ref[...]
    gk = gk_ref[...]
    ss_rope = jnp.sum(jnp.where(first, rope * rope, 0.0), axis=-1, keepdims=True)
    kr = rope * (gk[:, MLA_NOPE:] * tab)
    kr = kr + pltpu.roll(kr, MLA_ROPE, axis=1)
    q_rope_mul = gq[:, MLA_NOPE:] * tab * ATTN_SCALE
    q_nope_mul = gq[:, :MLA_NOPE] * ATTN_SCALE
    for h in range(MLA_HEADS):
        c0 = h * MLA_QK_PAD
        qa = q[:, c0:c0 + MLA_NOPE]
        qb = q[:, c0 + MLA_NOPE:c0 + MLA_QK_PAD]
        ss = (jnp.sum(qa * qa, axis=-1, keepdims=True)
              + jnp.sum(jnp.where(first, qb * qb, 0.0), axis=-1, keepdims=True))
        inv = lax.rsqrt(ss / MLA_QK + EPS)
        q_ref[:, c0:c0 + MLA_NOPE] = (qa * inv * q_nope_mul).astype(BF16)
        q_ref[:, c0 + MLA_NOPE:c0 + MLA_QK_PAD] = (qb * inv * q_rope_mul).astype(BF16)
        ka = kv[:, c0:c0 + MLA_NOPE]
        ssk = jnp.sum(ka * ka, axis=-1, keepdims=True) + ss_rope
        invk = lax.rsqrt(ssk / MLA_QK + EPS)
        k_ref[:, c0:c0 + MLA_NOPE] = (ka * invk * gk[:, :MLA_NOPE]).astype(BF16)
        k_ref[:, c0 + MLA_NOPE:c0 + MLA_QK_PAD] = (kr * invk).astype(BF16)
        v_ref[:, h * MLA_V:(h + 1) * MLA_V] = kv[:, c0 + MLA_NOPE:c0 + MLA_QK_PAD].astype(BF16)


def _swap_halves(v):
    half = v.shape[-1] // 2
    return jnp.concatenate([v[..., half:], v[..., :half]], axis=-1)


def _mla_prep(small, tab, q_norm_w, kv_norm_w, w_q_up, w_kv_up, qn_w, kn_w, *, tm):
    t = small.shape[0]
    wq = w_q_up.reshape(MLA_Q_RANK, MLA_HEADS, MLA_QK)
    wq = jnp.concatenate([wq, _swap_halves(wq[..., MLA_NOPE:])], axis=-1)
    wq = wq.reshape(MLA_Q_RANK, MLA_HEADS * MLA_QK_PAD).astype(BF16)
    wkv = w_kv_up.astype(BF16)
    row = lambda g: jnp.concatenate([g, _swap_halves(g[MLA_NOPE:])]).astype(F32).reshape(1, MLA_QK_PAD)
    n = MLA_HEADS * MLA_QK_PAD
    return _call(
        _mla_prep_kernel, name="mla_prep", grid=(t // tm,),
        in_specs=[pl.BlockSpec((tm, MLA_Q_RANK), lambda i: (i, SMALL_QLAT // MLA_Q_RANK)),
                  pl.BlockSpec((tm, MLA_KV_RANK), lambda i: (i, SMALL_KVLAT // MLA_KV_RANK)),
                  pl.BlockSpec((tm, LANES), lambda i: (i, SMALL_ROPE // LANES)),
                  pl.BlockSpec((tm, LANES), lambda i: (i, 0)),
                  pl.BlockSpec((1, MLA_Q_RANK), lambda i: (0, 0)),
                  pl.BlockSpec((1, MLA_KV_RANK), lambda i: (0, 0)),
                  pl.BlockSpec((MLA_Q_RANK, n), lambda i: (0, 0)),
                  pl.BlockSpec((MLA_KV_RANK, n), lambda i: (0, 0)),
                  pl.BlockSpec((1, MLA_QK_PAD), lambda i: (0, 0)),
                  pl.BlockSpec((1, MLA_QK_PAD), lambda i: (0, 0))],
        out_specs=[pl.BlockSpec((tm, n), lambda i: (i, 0)),
                   pl.BlockSpec((tm, n), lambda i: (i, 0)),
                   pl.BlockSpec((tm, MLA_OUT), lambda i: (i, 0))],
        out_shape=[jax.ShapeDtypeStruct((t, n), BF16),
                   jax.ShapeDtypeStruct((t, n), BF16),
                   jax.ShapeDtypeStruct((t, MLA_OUT), BF16)],
        semantics=("parallel",))(
            small, small, small, tab, q_norm_w.astype(F32).reshape(1, MLA_Q_RANK),
            kv_norm_w.astype(F32).reshape(1, MLA_KV_RANK), wq, wkv, row(qn_w), row(kn_w))


def _attn_kernel(q_ref, k_ref, v_ref, o_ref, *, tq):
    i = pl.program_id(2)
    q = q_ref[0]

    def tile(start, masked, carry):
        m, l, acc = carry
        k = k_ref[0, pl.ds(start, tq), :]
        v = v_ref[0, pl.ds(start, tq), :]
        s = lax.dot_general(q, k, (((1,), (1,)), ((), ())), preferred_element_type=F32)
        if masked:
            qc = lax.broadcasted_iota(jnp.int32, s.shape, 0) // CHUNK
            kc = lax.broadcasted_iota(jnp.int32, s.shape, 1) // CHUNK
            s = jnp.where(kc <= qc, s, -jnp.inf)
        m_new = jnp.maximum(m, jnp.max(s, axis=-1, keepdims=True))
        p = jnp.exp(s - m_new)
        alpha = jnp.exp(m - m_new)
        l = alpha * l + jnp.sum(p, axis=-1, keepdims=True)
        acc = alpha * acc + jnp.dot(p.astype(BF16), v, preferred_element_type=F32)
        return m_new, l, acc

    init = (jnp.full((tq, 1), -jnp.inf, F32), jnp.zeros((tq, 1), F32), jnp.zeros((tq, MLA_V), F32))
    carry = lax.fori_loop(0, i, lambda j, c: tile(pl.multiple_of(j * tq, tq), False, c), init)
    m, l, acc = tile(pl.multiple_of(i * tq, tq), True, carry)
    o_ref[0] = (acc / l).astype(o_ref.dtype)


def _attention(q, k, v, *, bsz, seq, tq):
    n = MLA_HEADS * MLA_QK_PAD
    kernel = functools.partial(_attn_kernel, tq=tq)
    out = _call(
        kernel, name="mla_attention", grid=(bsz, MLA_HEADS, seq // tq),
        in_specs=[pl.BlockSpec((1, tq, MLA_QK_PAD), lambda b, h, i: (b, i, h)),
                  pl.BlockSpec((1, seq, MLA_QK_PAD), lambda b, h, i: (b, 0, h)),
                  pl.BlockSpec((1, seq, MLA_V), lambda b, h, i: (b, 0, h))],
        out_specs=pl.BlockSpec((1, tq, MLA_V), lambda b, h, i: (b, i, h)),
        out_shape=jax.ShapeDtypeStruct((bsz, seq, MLA_OUT), BF16),
        semantics=("parallel", "parallel", "arbitrary"))(
            q.reshape(bsz, seq, n), k.reshape(bsz, seq, n), v.reshape(bsz, seq, MLA_OUT))
    return out.reshape(bsz * seq, MLA_OUT)


def _in_proj_weights(w_in):
    seg = lambda i: w_in[:, IN_OFFS[i]:IN_OFFS[i + 1]]
    z, xbc, dt, q_lat, kv_lat, k_rope, ga, gb = (seg(i) for i in range(8))
    main = jnp.concatenate([z, xbc, ga, gb], axis=1).astype(BF16)
    pad = jnp.zeros((w_in.shape[0], SMALL_DIM - SMALL_DT - SSM_HEADS), w_in.dtype)
    small = jnp.concatenate([q_lat, kv_lat, k_rope, _swap_halves(k_rope), dt, pad], axis=1).astype(BF16)
    return main, small


def kernel(x, c, positions, norm1_w, norm2_w, w_mod, b_mod, w_in, ssm_conv_w, ssm_conv_b, ssm_dt_bias, ssm_a_log, ssm_d, ssm_norm_w, w_ssm_out, mla_q_norm_w, w_q_up, mla_kv_norm_w, w_kv_up, qk_norm_q_w, qk_norm_k_w, w_mla_out, w_mix_out, w_ffn_up, ffn_conv_w, ffn_conv_b, w_ffn_down):
    bsz, seq, d = x.shape
    depth = w_mod.shape[0]
    t = bsz * seq
    mod = _modulation(c, w_mod, b_mod)
    tab = _rope_table(positions)
    x2 = x.reshape(t, d)
    row = lambda v: v.astype(F32).reshape(1, -1)
    for l in range(depth):
        shift1, scale1, gate1, shift2, scale2, gate2 = (m.reshape(bsz, 1, d) for m in jnp.split(mod[l], 6, axis=-1))

        w_main, w_small = _in_proj_weights(w_in[l])
        main = _norm_matmul(x2, row(norm1_w[l]), shift1, scale1, w_main, seq=seq, tm=1024, tn=1024,
                            out_dtype=BF16, name="in_proj_main")
        small = _norm_matmul(x2, row(norm1_w[l]), shift1, scale1, w_small, seq=seq, tm=1024, tn=SMALL_DIM,
                             out_dtype=F32, name="in_proj_small")
        xbc_act = _causal_conv_silu(main, ssm_conv_w[l].astype(F32), ssm_conv_b[l].astype(F32), seq=seq,
                                    col0=MAIN_XBC, width=SSM_CONV_DIM, val_col0=None, ts=512, tc=1024,
                                    name="ssm_conv")
        y_ssd = _ssd(xbc_act, main, small, ssm_dt_bias[l], ssm_a_log[l], ssm_d[l], ssm_norm_w[l],
                     bsz=bsz, seq=seq)
        q, k, v = _mla_prep(small, tab, mla_q_norm_w[l], mla_kv_norm_w[l], w_q_up[l], w_kv_up[l],
                            qk_norm_q_w[l], qk_norm_k_w[l], tm=256)
        o_mla = _attention(q, k, v, bsz=bsz, seq=seq, tq=512)
        merged = _merge(y_ssd, o_mla, w_ssm_out[l].astype(BF16), w_mla_out[l].astype(BF16), main, tm=512, tn=512)
        x2 = _matmul_residual(merged, w_mix_out[l].astype(BF16), x2, gate1, seq=seq, tm=1024, tn=512,
                              name="mix_out")

        u = _norm_matmul(x2, row(norm2_w[l]), shift2, scale2, w_ffn_up[l].astype(BF16), seq=seq, tm=1024,
                         tn=1024, out_dtype=BF16, name="ffn_up")
        act = _causal_conv_silu(u, ffn_conv_w[l].astype(F32), ffn_conv_b[l].astype(F32), seq=seq, col0=0,
                                width=FFN_DIM, val_col0=FFN_DIM, ts=512, tc=1408, name="ffn_conv_glu")
        x2 = _matmul_residual(act, w_ffn_down[l].astype(BF16), x2, gate2, seq=seq, tm=1024, tn=512,
                              name="ffn_down")
    return x2.reshape(bsz, seq, d)
```

```python
import functools
import math

import jax
import jax.numpy as jnp
import numpy as np
from jax import lax
from jax.experimental import pallas as pl
from jax.experimental.pallas import tpu as pltpu

F32 = jnp.float32
BF16 = jnp.bfloat16

D_MODEL = 2048
CHUNK = 64
EPS = 1e-6

SSM_D_INNER = 2 * D_MODEL
SSM_HEAD_DIM = 64
SSM_HEADS = SSM_D_INNER // SSM_HEAD_DIM
SSM_GROUPS = 8
SSM_HEADS_PER_GROUP = SSM_HEADS // SSM_GROUPS
SSM_STATE = 128
SSM_CONV = 4
SSM_BC = SSM_GROUPS * SSM_STATE
SSM_CONV_DIM = SSM_D_INNER + 2 * SSM_BC
SSM_GROUP_WIDTH = SSM_HEADS_PER_GROUP * SSM_HEAD_DIM

MLA_HEADS = 16
MLA_Q_RANK = 512
MLA_KV_RANK = 512
MLA_NOPE = 128
MLA_ROPE = 64
MLA_V = 128
MLA_QK = MLA_NOPE + MLA_ROPE
MLA_OUT = MLA_HEADS * MLA_V
MLA_QK_PAD = 256
ROPE_BASE = 10000.0
ATTN_SCALE = MLA_QK ** -0.5

FFN_DIM = 5632
FFN_CONV = 3

IN_SIZES = (SSM_D_INNER, SSM_CONV_DIM, SSM_HEADS, MLA_Q_RANK, MLA_KV_RANK, MLA_ROPE, D_MODEL, D_MODEL)
IN_OFFS = tuple(int(v) for v in np.cumsum((0,) + IN_SIZES))

MAIN_Z = 0
MAIN_XBC = SSM_D_INNER
MAIN_GA = MAIN_XBC + SSM_CONV_DIM
MAIN_GB = MAIN_GA + D_MODEL
MAIN_DIM = MAIN_GB + D_MODEL
SMALL_QLAT = 0
SMALL_KVLAT = MLA_Q_RANK
SMALL_ROPE = SMALL_KVLAT + MLA_KV_RANK
SMALL_DT = SMALL_ROPE + 2 * MLA_ROPE
SMALL_DIM = SMALL_DT + 128

LANES = 128
HALO = 16
VMEM_LIMIT_MB = 56


def _call(kernel, *, name, grid, in_specs, out_specs, out_shape, semantics, scratch=()):
    return pl.pallas_call(
        kernel, grid=grid, in_specs=in_specs, out_specs=out_specs, out_shape=out_shape,
        scratch_shapes=list(scratch), name=name,
        compiler_params=pltpu.CompilerParams(dimension_semantics=semantics,
                                             vmem_limit_bytes=VMEM_LIMIT_MB * 1024 * 1024))


def _silu(v):
    return v * jax.nn.sigmoid(v)


def _rms(v, w):
    ms = jnp.mean(v * v, axis=-1, keepdims=True)
    return v * lax.rsqrt(ms + EPS) * w


def _mod_kernel(c_ref, w_ref, b_ref, o_ref):
    cond = _silu(c_ref[...]).astype(BF16)
    o_ref[0] = jnp.dot(cond, w_ref[0].astype(BF16), preferred_element_type=F32) + b_ref[0]


def _modulation(c, w_mod, b_mod):
    depth, d, n = w_mod.shape
    bsz = c.shape[0]
    tn = 1024
    c8 = jnp.zeros((8, d), F32).at[:bsz].set(c)
    out = _call(
        _mod_kernel, name="adaln_mod", grid=(depth, n // tn),
        in_specs=[pl.BlockSpec((8, d), lambda l, j: (0, 0)),
                  pl.BlockSpec((1, d, tn), lambda l, j: (l, 0, j)),
                  pl.BlockSpec((1, 1, tn), lambda l, j: (l, 0, j))],
        out_specs=pl.BlockSpec((1, 8, tn), lambda l, j: (l, 0, j)),
        out_shape=jax.ShapeDtypeStruct((depth, 8, n), F32),
        semantics=("parallel", "parallel"))(c8, w_mod, b_mod.reshape(depth, 1, n))
    return out[:, :bsz]


def _rope_kernel(pos_ref, freq_ref, o_ref):
    ang = pos_ref[...].astype(F32) * freq_ref[...]
    lane = lax.broadcasted_iota(jnp.int32, ang.shape, 1)
    cos = jnp.cos(ang)
    sin = jnp.sin(ang)
    o_ref[...] = jnp.where(lane < 64, cos, jnp.where(lane < 96, -sin, sin))


def _rope_table(positions):
    t = positions.size
    rows = 2048
    half = MLA_ROPE // 2
    inv_freq = (ROPE_BASE ** (-np.arange(0, MLA_ROPE, 2, dtype=np.float32) / MLA_ROPE)).astype(np.float32)
    freq = jnp.asarray(np.tile(inv_freq, LANES // half)[None, :])
    return _call(
        _rope_kernel, name="rope_table", grid=(t // rows,),
        in_specs=[pl.BlockSpec((rows, 1), lambda i: (i, 0)),
                  pl.BlockSpec((1, LANES), lambda i: (0, 0))],
        out_specs=pl.BlockSpec((rows, LANES), lambda i: (i, 0)),
        out_shape=jax.ShapeDtypeStruct((t, LANES), F32),
        semantics=("parallel",))(positions.reshape(t, 1), freq)


def _norm_mm_kernel(x_ref, nw_ref, sh_ref, sc_ref, w_ref, o_ref, h_ref, *, tm, rows):
    @pl.when(pl.program_id(1) == 0)
    def _():
        nw = nw_ref[...]
        sc = 1.0 + sc_ref[0]
        sh = sh_ref[0]

        def body(r, carry):
            sl = pl.ds(pl.multiple_of(r * rows, rows), rows)
            h_ref[sl, :] = (_rms(x_ref[sl, :], nw) * sc + sh).astype(BF16)
            return carry

        lax.fori_loop(0, tm // rows, body, 0)

    o_ref[...] = jnp.dot(h_ref[...], w_ref[...], preferred_element_type=F32).astype(o_ref.dtype)


def _norm_matmul(x2, nw, shift, scale, w, *, seq, tm, tn, out_dtype, name):
    t, d = x2.shape
    n = w.shape[1]
    tpb = seq // tm
    kernel = functools.partial(_norm_mm_kernel, tm=tm, rows=256)
    return _call(
        kernel, name=name, grid=(t // tm, n // tn),
        in_specs=[pl.BlockSpec((tm, d), lambda i, j: (i, 0)),
                  pl.BlockSpec((1, d), lambda i, j: (0, 0)),
                  pl.BlockSpec((1, 1, d), lambda i, j: (i // tpb, 0, 0)),
                  pl.BlockSpec((1, 1, d), lambda i, j: (i // tpb, 0, 0)),
                  pl.BlockSpec((d, tn), lambda i, j: (0, j))],
        out_specs=pl.BlockSpec((tm, tn), lambda i, j: (i, j)),
        out_shape=jax.ShapeDtypeStruct((t, n), out_dtype),
        scratch=[pltpu.VMEM((tm, d), BF16)],
        semantics=("parallel", "arbitrary"))(x2, nw, shift, scale, w)


def _mm_resid_kernel(a_ref, w_ref, r_ref, g_ref, o_ref):
    acc = jnp.dot(a_ref[...], w_ref[...], preferred_element_type=F32)
    o_ref[...] = r_ref[...] + g_ref[0] * acc


def _matmul_residual(a, w, res, gate, *, seq, tm, tn, name):
    t, k = a.shape
    n = w.shape[1]
    tpb = seq // tm
    return _call(
        _mm_resid_kernel, name=name, grid=(t // tm, n // tn),
        in_specs=[pl.BlockSpec((tm, k), lambda i, j: (i, 0)),
                  pl.BlockSpec((k, tn), lambda i, j: (0, j)),
                  pl.BlockSpec((tm, tn), lambda i, j: (i, j)),
                  pl.BlockSpec((1, 1, tn), lambda i, j: (i // tpb, 0, j))],
        out_specs=pl.BlockSpec((tm, tn), lambda i, j: (i, j)),
        out_shape=jax.ShapeDtypeStruct((t, n), F32),
        semantics=("parallel", "parallel"))(a, w, res, gate)


def _merge_kernel(y_ref, o_ref, ws_ref, wm_ref, ga_ref, gb_ref, out_ref):
    ys = jnp.dot(y_ref[...], ws_ref[...], preferred_element_type=F32)
    ym = jnp.dot(o_ref[...], wm_ref[...], preferred_element_type=F32)
    ga = jax.nn.sigmoid(ga_ref[...].astype(F32))
    gb = jax.nn.sigmoid(gb_ref[...].astype(F32))
    out_ref[...] = (ga * ys + gb * ym).astype(out_ref.dtype)


def _merge(y_ssd, o_mla, w_ssm_out, w_mla_out, main, *, tm, tn):
    t = y_ssd.shape[0]
    n = w_ssm_out.shape[1]
    ga_blk = MAIN_GA // tn
    gb_blk = MAIN_GB // tn
    return _call(
        _merge_kernel, name="branch_merge", grid=(t // tm, n // tn),
        in_specs=[pl.BlockSpec((tm, y_ssd.shape[1]), lambda i, j: (i, 0)),
                  pl.BlockSpec((tm, o_mla.shape[1]), lambda i, j: (i, 0)),
                  pl.BlockSpec((w_ssm_out.shape[0], tn), lambda i, j: (0, j)),
                  pl.BlockSpec((w_mla_out.shape[0], tn), lambda i, j: (0, j)),
                  pl.BlockSpec((tm, tn), lambda i, j: (i, ga_blk + j)),
                  pl.BlockSpec((tm, tn), lambda i, j: (i, gb_blk + j))],
        out_specs=pl.BlockSpec((tm, tn), lambda i, j: (i, j)),
        out_shape=jax.ShapeDtypeStruct((t, n), BF16),
        semantics=("parallel", "parallel"))(y_ssd, o_mla, w_ssm_out, w_mla_out, main, main)


def _norm_mm_conv_kernel(x_ref, xh_ref, nw_ref, sh_ref, sc_ref, w_ref, *rest, tm, rows, taps, conv_lo, conv_hi,
                         glu, tiles_per_seq):
    if glu:
        wv_ref, cw_ref, cb_ref, o_ref, h_ref, gbuf = rest
    else:
        cw_ref, cb_ref, o_ref, h_ref, gbuf = rest
    i = pl.program_id(0)
    j = pl.program_id(1)

    @pl.when(j == 0)
    def _():
        nw = nw_ref[...]
        sc = 1.0 + sc_ref[0]
        sh = sh_ref[0]
        h_ref[0:HALO, :] = (_rms(xh_ref[...], nw) * sc + sh).astype(BF16)

        def body(r, carry):
            src = pl.ds(pl.multiple_of(r * rows, rows), rows)
            dst = pl.ds(pl.multiple_of(HALO + r * rows, HALO), rows)
            h_ref[dst, :] = (_rms(x_ref[src, :], nw) * sc + sh).astype(BF16)
            return carry

        lax.fori_loop(0, tm // rows, body, 0)

    def plain():
        o_ref[...] = jnp.dot(h_ref[HALO:, :], w_ref[...], preferred_element_type=F32).astype(o_ref.dtype)

    def conv():
        w = w_ref[...]
        at_seq_start = (i % tiles_per_seq) == 0
        g_halo = jnp.dot(h_ref[0:HALO, :], w, preferred_element_type=F32)
        g = jnp.dot(h_ref[HALO:, :], w, preferred_element_type=F32)
        gbuf[0:HALO, :] = jnp.where(at_seq_start, 0.0, g_halo)
        gbuf[HALO:, :] = g
        cw = cw_ref[...]
        acc = cb_ref[...] + cw[taps - 1:taps, :] * g
        for k in range(taps - 1):
            acc = acc + cw[k:k + 1, :] * gbuf[pl.ds(HALO - (taps - 1) + k, tm), :]
        y = _silu(acc)
        if glu:
            y = y * jnp.dot(h_ref[HALO:, :], wv_ref[...], preferred_element_type=F32)
        o_ref[...] = y.astype(o_ref.dtype)

    if glu:
        conv()
    else:
        in_conv = jnp.logical_and(j >= conv_lo, j < conv_hi)
        pl.when(in_conv)(conv)
        pl.when(jnp.logical_not(in_conv))(plain)


def _norm_matmul_conv(x2, nw, shift, scale, w, conv_w, conv_b, *, seq, tm, tn, n_out, conv_col0, val_col0, name):
    t, d = x2.shape
    taps, cwidth = conv_w.shape
    glu = val_col0 is not None
    tpb = seq // tm
    hb = tm // HALO
    conv_lo = conv_col0 // tn
    n_conv = cwidth // tn
    kernel = functools.partial(_norm_mm_conv_kernel, tm=tm, rows=256, taps=taps, conv_lo=conv_lo,
                               conv_hi=conv_lo + n_conv, glu=glu, tiles_per_seq=tpb)
    conv_idx = lambda i, j: (0, jnp.clip(j - conv_lo, 0, n_conv - 1))
    in_specs = [pl.BlockSpec((tm, d), lambda i, j: (i, 0)),
                pl.BlockSpec((HALO, d), lambda i, j: (jnp.maximum(i * hb - 1, 0), 0)),
                pl.BlockSpec((1, d), lambda i, j: (0, 0)),
                pl.BlockSpec((1, 1, d), lambda i, j: (i // tpb, 0, 0)),
                pl.BlockSpec((1, 1, d), lambda i, j: (i // tpb, 0, 0)),
                pl.BlockSpec((d, tn), lambda i, j: (0, j))]
    args = [x2, x2, nw, shift, scale, w]
    if glu:
        vb = val_col0 // tn
        in_specs.append(pl.BlockSpec((d, tn), lambda i, j: (0, vb + j)))
        args.append(w)
    in_specs += [pl.BlockSpec((taps, tn), conv_idx), pl.BlockSpec((1, tn), conv_idx)]
    args += [conv_w, conv_b.reshape(1, cwidth)]
    return _call(
        kernel, name=name, grid=(t // tm, n_out // tn),
        in_specs=in_specs,
        out_specs=pl.BlockSpec((tm, tn), lambda i, j: (i, j)),
        out_shape=jax.ShapeDtypeStruct((t, n_out), BF16),
        scratch=[pltpu.VMEM((tm + HALO, d), BF16), pltpu.VMEM((tm + HALO, tn), F32)],
        semantics=("parallel", "arbitrary"))(*args)


def _split3(v):
    hi = v.astype(BF16).astype(F32)
    r1 = v - hi
    mid = r1.astype(BF16).astype(F32)
    lo = (r1 - mid).astype(BF16).astype(F32)
    return hi, mid, lo


def _pack3(v, lane):
    hi, mid, lo = _split3(v)
    first = lane < CHUNK
    p1 = jnp.where(first, hi, pltpu.roll(mid, CHUNK, axis=1))
    p2 = jnp.where(first, lo, 0.0)
    return jnp.concatenate([p1, p2], axis=1).astype(BF16)


def _ssd_kernel(xa_ref, b_ref, c_ref, z_ref, dt_ref, dtb_ref, alog_ref, dskip_ref, nw_ref, tri_ref, rep_ref,
                o_ref, state, y_scr):
    q = CHUNK
    gw = SSM_GROUP_WIDTH

    @pl.when(pl.program_id(1) == 0)
    def _():
        state[...] = jnp.zeros_like(state)

    lane = lax.broadcasted_iota(jnp.int32, (q, LANES), 1)
    x_dt = dt_ref[...] + dtb_ref[...]
    dt = jnp.maximum(x_dt, 0.0) + jnp.log1p(jnp.exp(-jnp.abs(x_dt)))
    dta = dt * (-jnp.exp(alog_ref[...]))

    hi, mid, lo = _split3(dta)
    stacked = jnp.concatenate([hi, mid, lo, jnp.zeros_like(hi)], axis=0).astype(BF16)
    acs = jnp.dot(tri_ref[...], stacked, preferred_element_type=F32)

    lhs = jnp.concatenate([_pack3(acs, lane), _pack3(dt, lane)], axis=0)

    row = lax.broadcasted_iota(jnp.int32, (q, gw), 0)
    col = lax.broadcasted_iota(jnp.int32, (q, gw), 1) % q
    causal = col <= row
    diag = col == row
    bd_r = lax.broadcasted_iota(jnp.int32, (4 * q, 4 * q), 0) // q
    bd_c = lax.broadcasted_iota(jnp.int32, (4 * q, 4 * q), 1) // q
    blockdiag = bd_r == bd_c

    for g in range(SSM_GROUPS):
        gs = slice(g * gw, (g + 1) * gw)
        ns = slice(g * SSM_STATE, (g + 1) * SSM_STATE)
        rg = jnp.dot(lhs, rep_ref[:, gs], preferred_element_type=F32)
        colb = rg[:q]
        dtrep = rg[q:]
        acs_row = jnp.sum(jnp.where(diag, colb, 0.0), axis=0, keepdims=True)
        dt_row = jnp.sum(jnp.where(diag, dtrep, 0.0), axis=0, keepdims=True)
        bg = b_ref[:, ns]
        cg = c_ref[:, ns]
        b_tiled = jnp.concatenate([bg] * SSM_HEADS_PER_GROUP, axis=0)
        cb = lax.dot_general(cg, b_tiled, (((1,), (1,)), ((), ())), preferred_element_type=F32)
        lmask = jnp.exp(jnp.where(causal, colb - acs_row, -jnp.inf))
        m = (cb * lmask * dt_row).astype(BF16)
        xg = xa_ref[:, gs]
        y_parts = []
        for quad in range(2):
            xs = xg[:, quad * 4 * q:(quad + 1) * 4 * q]
            x_bd = jnp.where(blockdiag, jnp.concatenate([xs] * 4, axis=0), jnp.zeros((), BF16))
            y_parts.append(jnp.dot(m[:, quad * 4 * q:(quad + 1) * 4 * q], x_bd, preferred_element_type=F32))
        y_diag = jnp.concatenate(y_parts, axis=1)
        st = state[g]
        y_off = jnp.dot(cg, st.astype(BF16), preferred_element_type=F32) * jnp.exp(colb)
        last = colb[q - 1:q, :]
        xf = xg.astype(F32)
        xd = (xf * (jnp.exp(last - colb) * dtrep)).astype(BF16)
        state[g] = st * jnp.exp(last) + lax.dot_general(bg, xd, (((0,), (0,)), ((), ())),
                                                        preferred_element_type=F32)
        y_scr[:, gs] = y_diag + y_off + xf * dskip_ref[:, gs]

    gated = y_scr[...] * _silu(z_ref[...].astype(F32))
    o_ref[...] = _rms(gated, nw_ref[...]).astype(o_ref.dtype)


def _ssd_constants():
    q = CHUNK
    tri = np.tril(np.ones((q, q), np.float32))
    tri4 = np.concatenate([tri, tri, tri, np.zeros_like(tri)], axis=1)
    rep = np.zeros((4 * q, SSM_D_INNER), np.float32)
    for j in range(3):
        for r in range(SSM_HEADS):
            rep[j * q + r, r * SSM_HEAD_DIM:(r + 1) * SSM_HEAD_DIM] = 1.0
    return jnp.asarray(tri4, BF16), jnp.asarray(rep, BF16)


def _ssd(main, small, dt_bias, a_log, d_skip, norm_w, *, bsz, seq):
    t = bsz * seq
    nc = seq // CHUNK
    q = CHUNK
    tri4, rep = _ssd_constants()
    pad = lambda v: jnp.concatenate([v.astype(F32), jnp.zeros((LANES - SSM_HEADS,), F32)]).reshape(1, LANES)
    dskip_row = jnp.repeat(d_skip.astype(F32), SSM_HEAD_DIM).reshape(1, SSM_D_INNER)
    rowmap = lambda b, c: b * nc + c
    return _call(
        _ssd_kernel, name="ssd_scan", grid=(bsz, nc),
        in_specs=[pl.BlockSpec((q, SSM_D_INNER), lambda b, c: (rowmap(b, c), MAIN_XBC // SSM_D_INNER)),
                  pl.BlockSpec((q, SSM_BC), lambda b, c: (rowmap(b, c), (MAIN_XBC + SSM_D_INNER) // SSM_BC)),
                  pl.BlockSpec((q, SSM_BC), lambda b, c: (rowmap(b, c), (MAIN_XBC + SSM_D_INNER) // SSM_BC + 1)),
                  pl.BlockSpec((q, SSM_D_INNER), lambda b, c: (rowmap(b, c), MAIN_Z // SSM_D_INNER)),
                  pl.BlockSpec((q, LANES), lambda b, c: (rowmap(b, c), SMALL_DT // LANES)),
                  pl.BlockSpec((1, LANES), lambda b, c: (0, 0)),
                  pl.BlockSpec((1, LANES), lambda b, c: (0, 0)),
                  pl.BlockSpec((1, SSM_D_INNER), lambda b, c: (0, 0)),
                  pl.BlockSpec((1, SSM_D_INNER), lambda b, c: (0, 0)),
                  pl.BlockSpec((q, 4 * q), lambda b, c: (0, 0)),
                  pl.BlockSpec((4 * q, SSM_D_INNER), lambda b, c: (0, 0))],
        out_specs=pl.BlockSpec((q, SSM_D_INNER), lambda b, c: (rowmap(b, c), 0)),
        out_shape=jax.ShapeDtypeStruct((t, SSM_D_INNER), BF16),
        scratch=[pltpu.VMEM((SSM_GROUPS, SSM_STATE, SSM_GROUP_WIDTH), F32),
                 pltpu.VMEM((q, SSM_D_INNER), F32)],
        semantics=("parallel", "arbitrary"))(
            main, main, main, main, small, pad(dt_bias), pad(a_log), dskip_row,
            norm_w.astype(F32).reshape(1, SSM_D_INNER), tri4, rep)


def _mla_prep_kernel(ql_ref, kvl_ref, rope_ref, tab_ref, qnw_ref, kvnw_ref, wq_ref, wk_ref, wvt_ref, gq_ref, gk_ref,
                     q_ref, k_ref, vt_ref):
    qn = _rms(ql_ref[...], qnw_ref[...]).astype(BF16)
    kvn = _rms(kvl_ref[...], kvnw_ref[...]).astype(BF16)
    q = jnp.dot(qn, wq_ref[...], preferred_element_type=F32)
    kn = jnp.dot(kvn, wk_ref[...], preferred_element_type=F32)
    vt_ref[0, 0] = lax.dot_general(wvt_ref[...], kvn, (((1,), (1,)), ((), ())),
                                   preferred_element_type=F32).astype(BF16)
    tab = tab_ref[...]
    rope = rope_ref[...]
    lane = lax.broadcasted_iota(jnp.int32, rope.shape, 1)
    first = lane < MLA_ROPE
    gq = gq_ref[...]
    gk = gk_ref[...]
    ss_rope = jnp.sum(jnp.where(first, rope * rope, 0.0), axis=-1, keepdims=True)
    kr = rope * (gk[:, MLA_NOPE:] * tab)
    kr = kr + pltpu.roll(kr, MLA_ROPE, axis=1)
    q_rope_mul = gq[:, MLA_NOPE:] * tab * ATTN_SCALE
    q_nope_mul = gq[:, :MLA_NOPE] * ATTN_SCALE
    for h in range(MLA_HEADS):
        c0 = h * MLA_QK_PAD
        qa = q[:, c0:c0 + MLA_NOPE]
        qb = q[:, c0 + MLA_NOPE:c0 + MLA_QK_PAD]
        ss = (jnp.sum(qa * qa, axis=-1, keepdims=True)
              + jnp.sum(jnp.where(first, qb * qb, 0.0), axis=-1, keepdims=True))
        inv = lax.rsqrt(ss / MLA_QK + EPS)
        q_ref[:, c0:c0 + MLA_NOPE] = (qa * inv * q_nope_mul).astype(BF16)
        q_ref[:, c0 + MLA_NOPE:c0 + MLA_QK_PAD] = (qb * inv * q_rope_mul).astype(BF16)
        ka = kn[:, h * MLA_NOPE:(h + 1) * MLA_NOPE]
        ssk = jnp.sum(ka * ka, axis=-1, keepdims=True) + ss_rope
        invk = lax.rsqrt(ssk / MLA_QK + EPS)
        k_ref[:, c0:c0 + MLA_NOPE] = (ka * invk * gk[:, :MLA_NOPE]).astype(BF16)
        k_ref[:, c0 + MLA_NOPE:c0 + MLA_QK_PAD] = (kr * invk).astype(BF16)


def _swap_halves(v):
    half = v.shape[-1] // 2
    return jnp.concatenate([v[..., half:], v[..., :half]], axis=-1)


def _mla_prep(small, tab, q_norm_w, kv_norm_w, w_q_up, w_kv_up, qn_w, kn_w, *, bsz, seq, tm):
    t = small.shape[0]
    tpb = seq // tm
    wq = w_q_up.reshape(MLA_Q_RANK, MLA_HEADS, MLA_QK)
    wq = jnp.concatenate([wq, _swap_halves(wq[..., MLA_NOPE:])], axis=-1)
    wq = wq.reshape(MLA_Q_RANK, MLA_HEADS * MLA_QK_PAD).astype(BF16)
    wkv = w_kv_up.reshape(MLA_KV_RANK, MLA_HEADS, MLA_NOPE + MLA_V)
    wk = wkv[..., :MLA_NOPE].reshape(MLA_KV_RANK, MLA_HEADS * MLA_NOPE).astype(BF16)
    wvt = wkv[..., MLA_NOPE:].reshape(MLA_KV_RANK, MLA_OUT).T.astype(BF16)
    row = lambda g: jnp.concatenate([g, _swap_halves(g[MLA_NOPE:])]).astype(F32).reshape(1, MLA_QK_PAD)
    n = MLA_HEADS * MLA_QK_PAD
    return _call(
        _mla_prep_kernel, name="mla_prep", grid=(t // tm,),
        in_specs=[pl.BlockSpec((tm, MLA_Q_RANK), lambda i: (i, SMALL_QLAT // MLA_Q_RANK)),
                  pl.BlockSpec((tm, MLA_KV_RANK), lambda i: (i, SMALL_KVLAT // MLA_KV_RANK)),
                  pl.BlockSpec((tm, LANES), lambda i: (i, SMALL_ROPE // LANES)),
                  pl.BlockSpec((tm, LANES), lambda i: (i, 0)),
                  pl.BlockSpec((1, MLA_Q_RANK), lambda i: (0, 0)),
                  pl.BlockSpec((1, MLA_KV_RANK), lambda i: (0, 0)),
                  pl.BlockSpec((MLA_Q_RANK, n), lambda i: (0, 0)),
                  pl.BlockSpec((MLA_KV_RANK, MLA_HEADS * MLA_NOPE), lambda i: (0, 0)),
                  pl.BlockSpec((MLA_OUT, MLA_KV_RANK), lambda i: (0, 0)),
                  pl.BlockSpec((1, MLA_QK_PAD), lambda i: (0, 0)),
                  pl.BlockSpec((1, MLA_QK_PAD), lambda i: (0, 0))],
        out_specs=[pl.BlockSpec((tm, n), lambda i: (i, 0)),
                   pl.BlockSpec((tm, n), lambda i: (i, 0)),
                   pl.BlockSpec((1, 1, MLA_OUT, tm), lambda i: (i // tpb, i % tpb, 0, 0))],
        out_shape=[jax.ShapeDtypeStruct((t, n), BF16),
                   jax.ShapeDtypeStruct((t, n), BF16),
                   jax.ShapeDtypeStruct((bsz, tpb, MLA_OUT, tm), BF16)],
        semantics=("parallel",))(
            small, small, small, tab, q_norm_w.astype(F32).reshape(1, MLA_Q_RANK),
            kv_norm_w.astype(F32).reshape(1, MLA_KV_RANK), wq, wk, wvt, row(qn_w), row(kn_w))


def _attn_kernel(q_ref, k_ref, vt_ref, o_ref, *, tq):
    i = pl.program_id(2)
    q = q_ref[0]

    def scores(j):
        k = k_ref[0, pl.ds(pl.multiple_of(j * tq, tq), tq), :]
        return lax.dot_general(k, q, (((1,), (1,)), ((), ())), preferred_element_type=F32)

    def update(s, j, m, l, acc, masked):
        if masked:
            kc = lax.broadcasted_iota(jnp.int32, s.shape, 0) // CHUNK
            qc = lax.broadcasted_iota(jnp.int32, s.shape, 1) // CHUNK
            s = jnp.where(kc <= qc, s, -jnp.inf)
        m_new = jnp.maximum(m, jnp.max(s, axis=0, keepdims=True))
        p = jnp.exp(s - m_new)
        alpha = jnp.exp(m - m_new)
        l = alpha * l + jnp.sum(p, axis=0, keepdims=True)
        acc = alpha * acc + jnp.dot(vt_ref[0, j], p.astype(BF16), preferred_element_type=F32)
        return m_new, l, acc

    def body(j, carry):
        s, m, l, acc = carry
        s_next = scores(j + 1)
        m, l, acc = update(s, j, m, l, acc, False)
        return s_next, m, l, acc

    init = (scores(0), jnp.full((1, tq), -jnp.inf, F32), jnp.zeros((1, tq), F32), jnp.zeros((MLA_V, tq), F32))
    s, m, l, acc = lax.fori_loop(0, i, body, init)
    m, l, acc = update(s, i, m, l, acc, True)
    o_ref[0] = (acc / l).T.astype(o_ref.dtype)


def _attention(q, k, vt, *, bsz, seq, tq):
    n = MLA_HEADS * MLA_QK_PAD
    kernel = functools.partial(_attn_kernel, tq=tq)
    out = _call(
        kernel, name="mla_attention", grid=(bsz, MLA_HEADS, seq // tq),
        in_specs=[pl.BlockSpec((1, tq, MLA_QK_PAD), lambda b, h, i: (b, i, h)),
                  pl.BlockSpec((1, seq, MLA_QK_PAD), lambda b, h, i: (b, 0, h)),
                  pl.BlockSpec((1, seq // tq, MLA_V, tq), lambda b, h, i: (b, 0, h, 0))],
        out_specs=pl.BlockSpec((1, tq, MLA_V), lambda b, h, i: (b, i, h)),
        out_shape=jax.ShapeDtypeStruct((bsz, seq, MLA_OUT), BF16),
        semantics=("parallel", "parallel", "arbitrary"))(
            q.reshape(bsz, seq, n), k.reshape(bsz, seq, n), vt)
    return out.reshape(bsz * seq, MLA_OUT)


def _in_proj_weights(w_in):
    seg = lambda i: w_in[:, IN_OFFS[i]:IN_OFFS[i + 1]]
    z, xbc, dt, q_lat, kv_lat, k_rope, ga, gb = (seg(i) for i in range(8))
    main = jnp.concatenate([z, xbc, ga, gb], axis=1).astype(BF16)
    pad = jnp.zeros((w_in.shape[0], SMALL_DIM - SMALL_DT - SSM_HEADS), w_in.dtype)
    small = jnp.concatenate([q_lat, kv_lat, k_rope, _swap_halves(k_rope), dt, pad], axis=1).astype(BF16)
    return main, small


def kernel(x, c, positions, norm1_w, norm2_w, w_mod, b_mod, w_in, ssm_conv_w, ssm_conv_b, ssm_dt_bias, ssm_a_log, ssm_d, ssm_norm_w, w_ssm_out, mla_q_norm_w, w_q_up, mla_kv_norm_w, w_kv_up, qk_norm_q_w, qk_norm_k_w, w_mla_out, w_mix_out, w_ffn_up, ffn_conv_w, ffn_conv_b, w_ffn_down):
    bsz, seq, d = x.shape
    depth = w_mod.shape[0]
    t = bsz * seq
    mod = _modulation(c, w_mod, b_mod)
    tab = _rope_table(positions)
    x2 = x.reshape(t, d)
    row = lambda v: v.astype(F32).reshape(1, -1)
    for l in range(depth):
        shift1, scale1, gate1, shift2, scale2, gate2 = (m.reshape(bsz, 1, d) for m in jnp.split(mod[l], 6, axis=-1))

        w_main, w_small = _in_proj_weights(w_in[l])
        main = _norm_matmul_conv(x2, row(norm1_w[l]), shift1, scale1, w_main, ssm_conv_w[l].astype(F32),
                                 ssm_conv_b[l].astype(F32), seq=seq, tm=1024, tn=1024, n_out=MAIN_DIM,
                                 conv_col0=MAIN_XBC, val_col0=None, name="in_proj_main")
        small = _norm_matmul(x2, row(norm1_w[l]), shift1, scale1, w_small, seq=seq, tm=1024, tn=SMALL_DIM,
                             out_dtype=F32, name="in_proj_small")
        y_ssd = _ssd(main, small, ssm_dt_bias[l], ssm_a_log[l], ssm_d[l], ssm_norm_w[l], bsz=bsz, seq=seq)
        q, k, vt = _mla_prep(small, tab, mla_q_norm_w[l], mla_kv_norm_w[l], w_q_up[l], w_kv_up[l],
                             qk_norm_q_w[l], qk_norm_k_w[l], bsz=bsz, seq=seq, tm=512)
        o_mla = _attention(q, k, vt, bsz=bsz, seq=seq, tq=512)
        merged = _merge(y_ssd, o_mla, w_ssm_out[l].astype(BF16), w_mla_out[l].astype(BF16), main, tm=512, tn=512)
        x2 = _matmul_residual(merged, w_mix_out[l].astype(BF16), x2, gate1, seq=seq, tm=1024, tn=512,
                              name="mix_out")

        act = _norm_matmul_conv(x2, row(norm2_w[l]), shift2, scale2, w_ffn_up[l].astype(BF16),
                                ffn_conv_w[l].astype(F32), ffn_conv_b[l].astype(F32), seq=seq, tm=1024, tn=512,
                                n_out=FFN_DIM, conv_col0=0, val_col0=FFN_DIM, name="ffn_up_glu")
        x2 = _matmul_residual(act, w_ffn_down[l].astype(BF16), x2, gate2, seq=seq, tm=1024, tn=512,
                              name="ffn_down")
    return x2.reshape(bsz, seq, d)
```

```python
import functools
import math

import jax
import jax.numpy as jnp
import numpy as np
from jax import lax
from jax.experimental import pallas as pl
from jax.experimental.pallas import tpu as pltpu

F32 = jnp.float32
BF16 = jnp.bfloat16

D_MODEL = 2048
CHUNK = 64
EPS = 1e-6

SSM_D_INNER = 2 * D_MODEL
SSM_HEAD_DIM = 64
SSM_HEADS = SSM_D_INNER // SSM_HEAD_DIM
SSM_GROUPS = 8
SSM_HEADS_PER_GROUP = SSM_HEADS // SSM_GROUPS
SSM_STATE = 128
SSM_CONV = 4
SSM_BC = SSM_GROUPS * SSM_STATE
SSM_CONV_DIM = SSM_D_INNER + 2 * SSM_BC
SSM_GROUP_WIDTH = SSM_HEADS_PER_GROUP * SSM_HEAD_DIM

MLA_HEADS = 16
MLA_Q_RANK = 512
MLA_KV_RANK = 512
MLA_NOPE = 128
MLA_ROPE = 64
MLA_V = 128
MLA_QK = MLA_NOPE + MLA_ROPE
MLA_OUT = MLA_HEADS * MLA_V
MLA_QK_PAD = 256
ROPE_BASE = 10000.0
ATTN_SCALE = MLA_QK ** -0.5

FFN_DIM = 5632
FFN_CONV = 3

IN_SIZES = (SSM_D_INNER, SSM_CONV_DIM, SSM_HEADS, MLA_Q_RANK, MLA_KV_RANK, MLA_ROPE, D_MODEL, D_MODEL)
IN_OFFS = tuple(int(v) for v in np.cumsum((0,) + IN_SIZES))

MAIN_Z = 0
MAIN_XBC = SSM_D_INNER
MAIN_GA = MAIN_XBC + SSM_CONV_DIM
MAIN_GB = MAIN_GA + D_MODEL
MAIN_DIM = MAIN_GB + D_MODEL
SMALL_QLAT = 0
SMALL_KVLAT = MLA_Q_RANK
SMALL_ROPE = SMALL_KVLAT + MLA_KV_RANK
SMALL_DT = SMALL_ROPE + 2 * MLA_ROPE
SMALL_DIM = SMALL_DT + 128

LANES = 128
HALO = 16
LOG2E = math.log2(math.e)
VMEM_LIMIT_MB = 56


def _call(kernel, *, name, grid, in_specs, out_specs, out_shape, semantics, scratch=()):
    return pl.pallas_call(
        kernel, grid=grid, in_specs=in_specs, out_specs=out_specs, out_shape=out_shape,
        scratch_shapes=list(scratch), name=name,
        compiler_params=pltpu.CompilerParams(dimension_semantics=semantics,
                                             vmem_limit_bytes=VMEM_LIMIT_MB * 1024 * 1024))


def _silu(v):
    return v * jax.nn.sigmoid(v)


def _rms(v, w):
    ms = jnp.mean(v * v, axis=-1, keepdims=True)
    return v * lax.rsqrt(ms + EPS) * w


def _mod_kernel(c_ref, w_ref, b_ref, o_ref):
    cond = _silu(c_ref[...]).astype(BF16)
    o_ref[0] = jnp.dot(cond, w_ref[0].astype(BF16), preferred_element_type=F32) + b_ref[0]


def _modulation(c, w_mod, b_mod):
    depth, d, n = w_mod.shape
    bsz = c.shape[0]
    tn = 1024
    c8 = jnp.zeros((8, d), F32).at[:bsz].set(c)
    out = _call(
        _mod_kernel, name="adaln_mod", grid=(depth, n // tn),
        in_specs=[pl.BlockSpec((8, d), lambda l, j: (0, 0)),
                  pl.BlockSpec((1, d, tn), lambda l, j: (l, 0, j)),
                  pl.BlockSpec((1, 1, tn), lambda l, j: (l, 0, j))],
        out_specs=pl.BlockSpec((1, 8, tn), lambda l, j: (l, 0, j)),
        out_shape=jax.ShapeDtypeStruct((depth, 8, n), F32),
        semantics=("parallel", "parallel"))(c8, w_mod, b_mod.reshape(depth, 1, n))
    return out[:, :bsz]


def _rope_kernel(pos_ref, freq_ref, o_ref):
    ang = pos_ref[...].astype(F32) * freq_ref[...]
    lane = lax.broadcasted_iota(jnp.int32, ang.shape, 1)
    cos = jnp.cos(ang)
    sin = jnp.sin(ang)
    o_ref[...] = jnp.where(lane < 64, cos, jnp.where(lane < 96, -sin, sin))


def _rope_table(positions):
    t = positions.size
    rows = 2048
    half = MLA_ROPE // 2
    inv_freq = (ROPE_BASE ** (-np.arange(0, MLA_ROPE, 2, dtype=np.float32) / MLA_ROPE)).astype(np.float32)
    freq = jnp.asarray(np.tile(inv_freq, LANES // half)[None, :])
    return _call(
        _rope_kernel, name="rope_table", grid=(t // rows,),
        in_specs=[pl.BlockSpec((rows, 1), lambda i: (i, 0)),
                  pl.BlockSpec((1, LANES), lambda i: (0, 0))],
        out_specs=pl.BlockSpec((rows, LANES), lambda i: (i, 0)),
        out_shape=jax.ShapeDtypeStruct((t, LANES), F32),
        semantics=("parallel",))(positions.reshape(t, 1), freq)


def _norm_mm_kernel(x_ref, nw_ref, sh_ref, sc_ref, w_ref, o_ref, h_ref, *, tm, rows):
    @pl.when(pl.program_id(1) == 0)
    def _():
        nw = nw_ref[...]
        sc = 1.0 + sc_ref[0]
        sh = sh_ref[0]

        def body(r, carry):
            sl = pl.ds(pl.multiple_of(r * rows, rows), rows)
            h_ref[sl, :] = (_rms(x_ref[sl, :], nw) * sc + sh).astype(BF16)
            return carry

        lax.fori_loop(0, tm // rows, body, 0)

    o_ref[...] = jnp.dot(h_ref[...], w_ref[...], preferred_element_type=F32).astype(o_ref.dtype)


def _norm_matmul(x2, nw, shift, scale, w, *, seq, tm, tn, out_dtype, name):
    t, d = x2.shape
    n = w.shape[1]
    tpb = seq // tm
    kernel = functools.partial(_norm_mm_kernel, tm=tm, rows=256)
    return _call(
        kernel, name=name, grid=(t // tm, n // tn),
        in_specs=[pl.BlockSpec((tm, d), lambda i, j: (i, 0)),
                  pl.BlockSpec((1, d), lambda i, j: (0, 0)),
                  pl.BlockSpec((1, 1, d), lambda i, j: (i // tpb, 0, 0)),
                  pl.BlockSpec((1, 1, d), lambda i, j: (i // tpb, 0, 0)),
                  pl.BlockSpec((d, tn), lambda i, j: (0, j))],
        out_specs=pl.BlockSpec((tm, tn), lambda i, j: (i, j)),
        out_shape=jax.ShapeDtypeStruct((t, n), out_dtype),
        scratch=[pltpu.VMEM((tm, d), BF16)],
        semantics=("parallel", "arbitrary"))(x2, nw, shift, scale, w)


def _mm_resid_kernel(a_ref, w_ref, r_ref, g_ref, o_ref):
    acc = jnp.dot(a_ref[...], w_ref[...], preferred_element_type=F32)
    o_ref[...] = r_ref[...] + g_ref[0] * acc


def _matmul_residual(a, w, res, gate, *, seq, tm, tn, name):
    t, k = a.shape
    n = w.shape[1]
    tpb = seq // tm
    return _call(
        _mm_resid_kernel, name=name, grid=(t // tm, n // tn),
        in_specs=[pl.BlockSpec((tm, k), lambda i, j: (i, 0)),
                  pl.BlockSpec((k, tn), lambda i, j: (0, j)),
                  pl.BlockSpec((tm, tn), lambda i, j: (i, j)),
                  pl.BlockSpec((1, 1, tn), lambda i, j: (i // tpb, 0, j))],
        out_specs=pl.BlockSpec((tm, tn), lambda i, j: (i, j)),
        out_shape=jax.ShapeDtypeStruct((t, n), F32),
        semantics=("parallel", "parallel"))(a, w, res, gate)


def _merge_kernel(y_ref, o_ref, ws_ref, wm_ref, ga_ref, gb_ref, out_ref):
    ys = jnp.dot(y_ref[...], ws_ref[...], preferred_element_type=F32)
    ym = jnp.dot(o_ref[...], wm_ref[...], preferred_element_type=F32)
    ga = jax.nn.sigmoid(ga_ref[...].astype(F32))
    gb = jax.nn.sigmoid(gb_ref[...].astype(F32))
    out_ref[...] = (ga * ys + gb * ym).astype(out_ref.dtype)


def _merge(y_ssd, o_mla, w_ssm_out, w_mla_out, main, *, tm, tn):
    t = y_ssd.shape[0]
    n = w_ssm_out.shape[1]
    ga_blk = MAIN_GA // tn
    gb_blk = MAIN_GB // tn
    return _call(
        _merge_kernel, name="branch_merge", grid=(t // tm, n // tn),
        in_specs=[pl.BlockSpec((tm, y_ssd.shape[1]), lambda i, j: (i, 0)),
                  pl.BlockSpec((tm, o_mla.shape[1]), lambda i, j: (i, 0)),
                  pl.BlockSpec((w_ssm_out.shape[0], tn), lambda i, j: (0, j)),
                  pl.BlockSpec((w_mla_out.shape[0], tn), lambda i, j: (0, j)),
                  pl.BlockSpec((tm, tn), lambda i, j: (i, ga_blk + j)),
                  pl.BlockSpec((tm, tn), lambda i, j: (i, gb_blk + j))],
        out_specs=pl.BlockSpec((tm, tn), lambda i, j: (i, j)),
        out_shape=jax.ShapeDtypeStruct((t, n), BF16),
        semantics=("parallel", "parallel"))(y_ssd, o_mla, w_ssm_out, w_mla_out, main, main)


def _norm_mm_conv_kernel(x_ref, xh_ref, nw_ref, sh_ref, sc_ref, w_ref, *rest, tm, rows, taps, conv_lo, conv_hi,
                         glu, tiles_per_seq):
    if glu:
        wv_ref, cw_ref, cb_ref, o_ref, h_ref, gbuf = rest
    else:
        cw_ref, cb_ref, o_ref, h_ref, gbuf = rest
    i = pl.program_id(0)
    j = pl.program_id(1)

    @pl.when(j == 0)
    def _():
        nw = nw_ref[...]
        sc = 1.0 + sc_ref[0]
        sh = sh_ref[0]
        h_ref[0:HALO, :] = (_rms(xh_ref[...], nw) * sc + sh).astype(BF16)

        def body(r, carry):
            src = pl.ds(pl.multiple_of(r * rows, rows), rows)
            dst = pl.ds(pl.multiple_of(HALO + r * rows, HALO), rows)
            h_ref[dst, :] = (_rms(x_ref[src, :], nw) * sc + sh).astype(BF16)
            return carry

        lax.fori_loop(0, tm // rows, body, 0)

    def plain():
        o_ref[...] = jnp.dot(h_ref[HALO:, :], w_ref[...], preferred_element_type=F32).astype(o_ref.dtype)

    def conv():
        at_seq_start = (i % tiles_per_seq) == 0
        cw = cw_ref[...]
        cbias = cb_ref[...]
        w = w_ref[...]
        g_halo = jnp.dot(h_ref[0:HALO, :], w, preferred_element_type=F32)
        g = jnp.dot(h_ref[HALO:, :], w, preferred_element_type=F32)
        gbuf[0:HALO, :] = jnp.where(at_seq_start, 0.0, g_halo)
        gbuf[HALO:, :] = g
        acc = cbias + cw[taps - 1:taps, :] * g
        for k in range(taps - 1):
            acc = acc + cw[k:k + 1, :] * gbuf[pl.ds(HALO - (taps - 1) + k, tm), :]
        y = _silu(acc)
        if glu:
            y = y * jnp.dot(h_ref[HALO:, :], wv_ref[...], preferred_element_type=F32)
        o_ref[...] = y.astype(o_ref.dtype)

    if glu:
        conv()
    else:
        in_conv = jnp.logical_and(j >= conv_lo, j < conv_hi)
        pl.when(in_conv)(conv)
        pl.when(jnp.logical_not(in_conv))(plain)


def _norm_matmul_conv(x2, nw, shift, scale, w, conv_w, conv_b, *, seq, tm, tn, n_out, conv_col0, val_col0, name):
    t, d = x2.shape
    taps, cwidth = conv_w.shape
    glu = val_col0 is not None
    tpb = seq // tm
    hb = tm // HALO
    conv_lo = conv_col0 // tn
    n_conv = cwidth // tn
    kernel = functools.partial(_norm_mm_conv_kernel, tm=tm, rows=256, taps=taps, conv_lo=conv_lo,
                               conv_hi=conv_lo + n_conv, glu=glu, tiles_per_seq=tpb)
    conv_idx = lambda i, j: (0, jnp.clip(j - conv_lo, 0, n_conv - 1))
    in_specs = [pl.BlockSpec((tm, d), lambda i, j: (i, 0)),
                pl.BlockSpec((HALO, d), lambda i, j: (jnp.maximum(i * hb - 1, 0), 0)),
                pl.BlockSpec((1, d), lambda i, j: (0, 0)),
                pl.BlockSpec((1, 1, d), lambda i, j: (i // tpb, 0, 0)),
                pl.BlockSpec((1, 1, d), lambda i, j: (i // tpb, 0, 0)),
                pl.BlockSpec((d, tn), lambda i, j: (0, j))]
    args = [x2, x2, nw, shift, scale, w]
    if glu:
        vb = val_col0 // tn
        in_specs.append(pl.BlockSpec((d, tn), lambda i, j: (0, vb + j)))
        args.append(w)
    in_specs += [pl.BlockSpec((taps, tn), conv_idx), pl.BlockSpec((1, tn), conv_idx)]
    args += [conv_w, conv_b.reshape(1, cwidth)]
    scratch = [pltpu.VMEM((tm + HALO, d), BF16), pltpu.VMEM((tm + HALO, tn), F32)]
    return _call(
        kernel, name=name, grid=(t // tm, n_out // tn),
        in_specs=in_specs,
        out_specs=pl.BlockSpec((tm, tn), lambda i, j: (i, j)),
        out_shape=jax.ShapeDtypeStruct((t, n_out), BF16),
        scratch=scratch,
        semantics=("parallel", "arbitrary"))(*args)


def _split3(v):
    hi = v.astype(BF16).astype(F32)
    r1 = v - hi
    mid = r1.astype(BF16).astype(F32)
    lo = (r1 - mid).astype(BF16).astype(F32)
    return hi, mid, lo


def _pack3(v, lane):
    hi, mid, lo = _split3(v)
    first = lane < CHUNK
    p1 = jnp.where(first, hi, pltpu.roll(mid, CHUNK, axis=1))
    p2 = jnp.where(first, lo, 0.0)
    return jnp.concatenate([p1, p2], axis=1).astype(BF16)


def _ssd_kernel(xa_ref, b_ref, c_ref, z_ref, dt_ref, dtb_ref, alog_ref, dskip_ref, nw_ref, tri_ref, rep_ref,
                o_ref, state, y_scr):
    q = CHUNK
    gw = SSM_GROUP_WIDTH

    @pl.when(pl.program_id(1) == 0)
    def _():
        state[...] = jnp.zeros_like(state)

    lane = lax.broadcasted_iota(jnp.int32, (q, LANES), 1)
    x_dt = dt_ref[...] + dtb_ref[...]
    dt = jnp.maximum(x_dt, 0.0) + jnp.log1p(jnp.exp(-jnp.abs(x_dt)))
    dta = dt * (-jnp.exp(alog_ref[...])) * LOG2E

    hi, mid, lo = _split3(dta)
    stacked = jnp.concatenate([hi, mid, lo, jnp.zeros_like(hi)], axis=0).astype(BF16)
    acs = jnp.dot(tri_ref[...], stacked, preferred_element_type=F32)

    lhs = jnp.concatenate([_pack3(acs, lane), _pack3(dt, lane)], axis=0)

    row = lax.broadcasted_iota(jnp.int32, (q, gw), 0)
    col = lax.broadcasted_iota(jnp.int32, (q, gw), 1) % q
    causal_bias = jnp.where(col <= row, 0.0, -jnp.inf)
    diag = col == row
    bd_r = lax.broadcasted_iota(jnp.int32, (4 * q, 4 * q), 0) // q
    bd_c = lax.broadcasted_iota(jnp.int32, (4 * q, 4 * q), 1) // q
    blockdiag = bd_r == bd_c

    for g in range(SSM_GROUPS):
        gs = slice(g * gw, (g + 1) * gw)
        ns = slice(g * SSM_STATE, (g + 1) * SSM_STATE)
        rg = jnp.dot(lhs, rep_ref[:, gs], preferred_element_type=F32)
        colb = rg[:q]
        dtrep = rg[q:]
        acs_row = jnp.sum(jnp.where(diag, colb, 0.0), axis=0, keepdims=True)
        dt_row = jnp.sum(jnp.where(diag, dtrep, 0.0), axis=0, keepdims=True)
        bg = b_ref[:, ns]
        cg = c_ref[:, ns]
        b_tiled = jnp.concatenate([bg] * SSM_HEADS_PER_GROUP, axis=0)
        cb = lax.dot_general(cg, b_tiled, (((1,), (1,)), ((), ())), preferred_element_type=F32)
        lmask = jnp.exp2(colb - acs_row + causal_bias)
        m = (cb * lmask * dt_row).astype(BF16)
        xg = xa_ref[:, gs]
        y_parts = []
        for quad in range(2):
            xs = xg[:, quad * 4 * q:(quad + 1) * 4 * q]
            x_bd = jnp.where(blockdiag, jnp.concatenate([xs] * 4, axis=0), jnp.zeros((), BF16))
            y_parts.append(jnp.dot(m[:, quad * 4 * q:(quad + 1) * 4 * q], x_bd, preferred_element_type=F32))
        y_diag = jnp.concatenate(y_parts, axis=1)
        st = state[g]
        y_off = jnp.dot(cg, st.astype(BF16), preferred_element_type=F32) * jnp.exp2(colb)
        last = colb[q - 1:q, :]
        xf = xg.astype(F32)
        xd = (xf * (jnp.exp2(last - colb) * dtrep)).astype(BF16)
        state[g] = st * jnp.exp2(last) + lax.dot_general(bg, xd, (((0,), (0,)), ((), ())),
                                                         preferred_element_type=F32)
        y_scr[:, gs] = y_diag + y_off + xf * dskip_ref[:, gs]

    gated = y_scr[...] * _silu(z_ref[...].astype(F32))
    o_ref[...] = _rms(gated, nw_ref[...]).astype(o_ref.dtype)


def _ssd_constants():
    q = CHUNK
    tri = np.tril(np.ones((q, q), np.float32))
    tri4 = np.concatenate([tri, tri, tri, np.zeros_like(tri)], axis=1)
    rep = np.zeros((4 * q, SSM_D_INNER), np.float32)
    for j in range(3):
        for r in range(SSM_HEADS):
            rep[j * q + r, r * SSM_HEAD_DIM:(r + 1) * SSM_HEAD_DIM] = 1.0
    return jnp.asarray(tri4, BF16), jnp.asarray(rep, BF16)


def _ssd(main, small, dt_bias, a_log, d_skip, norm_w, *, bsz, seq):
    t = bsz * seq
    nc = seq // CHUNK
    q = CHUNK
    tri4, rep = _ssd_constants()
    pad = lambda v: jnp.concatenate([v.astype(F32), jnp.zeros((LANES - SSM_HEADS,), F32)]).reshape(1, LANES)
    dskip_row = jnp.repeat(d_skip.astype(F32), SSM_HEAD_DIM).reshape(1, SSM_D_INNER)
    rowmap = lambda b, c: b * nc + c
    return _call(
        _ssd_kernel, name="ssd_scan", grid=(bsz, nc),
        in_specs=[pl.BlockSpec((q, SSM_D_INNER), lambda b, c: (rowmap(b, c), MAIN_XBC // SSM_D_INNER)),
                  pl.BlockSpec((q, SSM_BC), lambda b, c: (rowmap(b, c), (MAIN_XBC + SSM_D_INNER) // SSM_BC)),
                  pl.BlockSpec((q, SSM_BC), lambda b, c: (rowmap(b, c), (MAIN_XBC + SSM_D_INNER) // SSM_BC + 1)),
                  pl.BlockSpec((q, SSM_D_INNER), lambda b, c: (rowmap(b, c), MAIN_Z // SSM_D_INNER)),
                  pl.BlockSpec((q, LANES), lambda b, c: (rowmap(b, c), SMALL_DT // LANES)),
                  pl.BlockSpec((1, LANES), lambda b, c: (0, 0)),
                  pl.BlockSpec((1, LANES), lambda b, c: (0, 0)),
                  pl.BlockSpec((1, SSM_D_INNER), lambda b, c: (0, 0)),
                  pl.BlockSpec((1, SSM_D_INNER), lambda b, c: (0, 0)),
                  pl.BlockSpec((q, 4 * q), lambda b, c: (0, 0)),
                  pl.BlockSpec((4 * q, SSM_D_INNER), lambda b, c: (0, 0))],
        out_specs=pl.BlockSpec((q, SSM_D_INNER), lambda b, c: (rowmap(b, c), 0)),
        out_shape=jax.ShapeDtypeStruct((t, SSM_D_INNER), BF16),
        scratch=[pltpu.VMEM((SSM_GROUPS, SSM_STATE, SSM_GROUP_WIDTH), F32),
                 pltpu.VMEM((q, SSM_D_INNER), F32)],
        semantics=("parallel", "arbitrary"))(
            main, main, main, main, small, pad(dt_bias), pad(a_log), dskip_row,
            norm_w.astype(F32).reshape(1, SSM_D_INNER), tri4, rep)


def _mla_prep_kernel(ql_ref, kvl_ref, rope_ref, tab_ref, qnw_ref, kvnw_ref, wq_ref, wk_ref, wvt_ref, gq_ref, gk_ref,
                     q_ref, k_ref, vt_ref):
    qn = _rms(ql_ref[...], qnw_ref[...]).astype(BF16)
    kvn = _rms(kvl_ref[...], kvnw_ref[...]).astype(BF16)
    q = jnp.dot(qn, wq_ref[...], preferred_element_type=F32)
    kn = jnp.dot(kvn, wk_ref[...], preferred_element_type=F32)
    vt_ref[0, 0] = lax.dot_general(wvt_ref[...], kvn, (((1,), (1,)), ((), ())),
                                   preferred_element_type=F32).astype(BF16)
    tab = tab_ref[...]
    rope = rope_ref[...]
    lane = lax.broadcasted_iota(jnp.int32, rope.shape, 1)
    first = lane < MLA_ROPE
    gq = gq_ref[...]
    gk = gk_ref[...]
    ss_rope = jnp.sum(jnp.where(first, rope * rope, 0.0), axis=-1, keepdims=True)
    kr = rope * (gk[:, MLA_NOPE:] * tab)
    kr = kr + pltpu.roll(kr, MLA_ROPE, axis=1)
    q_rope_mul = gq[:, MLA_NOPE:] * tab * ATTN_SCALE
    q_nope_mul = gq[:, :MLA_NOPE] * ATTN_SCALE
    for h in range(MLA_HEADS):
        c0 = h * MLA_QK_PAD
        qa = q[:, c0:c0 + MLA_NOPE]
        qb = q[:, c0 + MLA_NOPE:c0 + MLA_QK_PAD]
        ss = (jnp.sum(qa * qa, axis=-1, keepdims=True)
              + jnp.sum(jnp.where(first, qb * qb, 0.0), axis=-1, keepdims=True))
        inv = lax.rsqrt(ss / MLA_QK + EPS)
        q_ref[:, c0:c0 + MLA_NOPE] = (qa * inv * q_nope_mul).astype(BF16)
        q_ref[:, c0 + MLA_NOPE:c0 + MLA_QK_PAD] = (qb * inv * q_rope_mul).astype(BF16)
        ka = kn[:, h * MLA_NOPE:(h + 1) * MLA_NOPE]
        ssk = jnp.sum(ka * ka, axis=-1, keepdims=True) + ss_rope
        invk = lax.rsqrt(ssk / MLA_QK + EPS)
        k_ref[:, c0:c0 + MLA_NOPE] = (ka * invk * gk[:, :MLA_NOPE]).astype(BF16)
        k_ref[:, c0 + MLA_NOPE:c0 + MLA_QK_PAD] = (kr * invk).astype(BF16)


def _swap_halves(v):
    half = v.shape[-1] // 2
    return jnp.concatenate([v[..., half:], v[..., :half]], axis=-1)


def _mla_prep(small, tab, q_norm_w, kv_norm_w, w_q_up, w_kv_up, qn_w, kn_w, *, bsz, seq, tm):
    t = small.shape[0]
    tpb = seq // tm
    wq = w_q_up.reshape(MLA_Q_RANK, MLA_HEADS, MLA_QK)
    wq = jnp.concatenate([wq, _swap_halves(wq[..., MLA_NOPE:])], axis=-1)
    wq = wq.reshape(MLA_Q_RANK, MLA_HEADS * MLA_QK_PAD).astype(BF16)
    wkv = w_kv_up.reshape(MLA_KV_RANK, MLA_HEADS, MLA_NOPE + MLA_V)
    wk = wkv[..., :MLA_NOPE].reshape(MLA_KV_RANK, MLA_HEADS * MLA_NOPE).astype(BF16)
    wvt = wkv[..., MLA_NOPE:].reshape(MLA_KV_RANK, MLA_OUT).T.astype(BF16)
    row = lambda g: jnp.concatenate([g, _swap_halves(g[MLA_NOPE:])]).astype(F32).reshape(1, MLA_QK_PAD)
    n = MLA_HEADS * MLA_QK_PAD
    return _call(
        _mla_prep_kernel, name="mla_prep", grid=(t // tm,),
        in_specs=[pl.BlockSpec((tm, MLA_Q_RANK), lambda i: (i, SMALL_QLAT // MLA_Q_RANK)),
                  pl.BlockSpec((tm, MLA_KV_RANK), lambda i: (i, SMALL_KVLAT // MLA_KV_RANK)),
                  pl.BlockSpec((tm, LANES), lambda i: (i, SMALL_ROPE // LANES)),
                  pl.BlockSpec((tm, LANES), lambda i: (i, 0)),
                  pl.BlockSpec((1, MLA_Q_RANK), lambda i: (0, 0)),
                  pl.BlockSpec((1, MLA_KV_RANK), lambda i: (0, 0)),
                  pl.BlockSpec((MLA_Q_RANK, n), lambda i: (0, 0)),
                  pl.BlockSpec((MLA_KV_RANK, MLA_HEADS * MLA_NOPE), lambda i: (0, 0)),
                  pl.BlockSpec((MLA_OUT, MLA_KV_RANK), lambda i: (0, 0)),
                  pl.BlockSpec((1, MLA_QK_PAD), lambda i: (0, 0)),
                  pl.BlockSpec((1, MLA_QK_PAD), lambda i: (0, 0))],
        out_specs=[pl.BlockSpec((tm, n), lambda i: (i, 0)),
                   pl.BlockSpec((tm, n), lambda i: (i, 0)),
                   pl.BlockSpec((1, 1, MLA_OUT, tm), lambda i: (i // tpb, i % tpb, 0, 0))],
        out_shape=[jax.ShapeDtypeStruct((t, n), BF16),
                   jax.ShapeDtypeStruct((t, n), BF16),
                   jax.ShapeDtypeStruct((bsz, tpb, MLA_OUT, tm), BF16)],
        semantics=("parallel",))(
            small, small, small, tab, q_norm_w.astype(F32).reshape(1, MLA_Q_RANK),
            kv_norm_w.astype(F32).reshape(1, MLA_KV_RANK), wq, wk, wvt, row(qn_w), row(kn_w))


def _attn_kernel(q_ref, k_ref, vt_ref, o_ref, s_buf, p_buf, *, tq, nq):
    n_tiles = nq * (nq + 1) // 2
    nt_dims = (((1,), (1,)), ((), ()))

    def rows(idx):
        return pl.ds(pl.multiple_of(idx * tq, tq), tq)

    def finish(qi_p, kj_p, alpha_p, l_p, acc, slot_p):
        acc = alpha_p * acc + jnp.dot(vt_ref[0, kj_p], p_buf[slot_p], preferred_element_type=F32)
        o_ref[0, rows(qi_p), :] = (acc / l_p).T.astype(o_ref.dtype)
        return acc

    def step(t, carry, *, masked):
        qi, kj, qi_p, kj_p, m, l, alpha_p, l_p, acc = carry
        slot = t % 2
        acc = finish(qi_p, kj_p, alpha_p, l_p, acc, 1 - slot)

        s = s_buf[slot]
        if masked:
            kc = lax.broadcasted_iota(jnp.int32, s.shape, 0) // CHUNK
            qc = lax.broadcasted_iota(jnp.int32, s.shape, 1) // CHUNK
            s = jnp.where(kc <= qc, s, -jnp.inf)
        m = jnp.where(kj == 0, -jnp.inf, m)
        m_new = jnp.maximum(m, jnp.max(s, axis=0, keepdims=True))
        p = jnp.exp(s - m_new)
        alpha = jnp.exp(m - m_new)
        l = alpha * l + jnp.sum(p, axis=0, keepdims=True)
        p_buf[slot] = p.astype(BF16)

        last = kj == qi
        qi_n = jnp.minimum(jnp.where(last, qi + 1, qi), nq - 1)
        kj_n = jnp.where(last, 0, kj + 1)
        s_buf[1 - slot] = lax.dot_general(k_ref[0, rows(kj_n), :], q_ref[0, rows(qi_n), :], nt_dims,
                                          preferred_element_type=F32)
        return qi_n, kj_n, qi, kj, m_new, l, alpha, l, acc

    def body(t, carry):
        return lax.cond(carry[0] == carry[1], functools.partial(step, masked=True),
                        functools.partial(step, masked=False), t, carry)

    s_buf[0] = lax.dot_general(k_ref[0, rows(0), :], q_ref[0, rows(0), :], nt_dims, preferred_element_type=F32)
    p_buf[1] = jnp.zeros(p_buf.shape[1:], BF16)
    zero = jnp.zeros((), jnp.int32)
    init = (zero, zero, zero, zero, jnp.full((1, tq), -jnp.inf, F32), jnp.zeros((1, tq), F32),
            jnp.zeros((1, tq), F32), jnp.ones((1, tq), F32), jnp.zeros((MLA_V, tq), F32))
    _, _, qi_p, kj_p, _, _, alpha_p, l_p, acc = lax.fori_loop(0, n_tiles, body, init)
    finish(qi_p, kj_p, alpha_p, l_p, acc, (n_tiles - 1) % 2)


def _attention(q, k, vt, *, bsz, seq, tq):
    n = MLA_HEADS * MLA_QK_PAD
    nq = seq // tq
    kernel = functools.partial(_attn_kernel, tq=tq, nq=nq)
    out = _call(
        kernel, name="mla_attention", grid=(bsz, MLA_HEADS),
        in_specs=[pl.BlockSpec((1, seq, MLA_QK_PAD), lambda b, h: (b, 0, h)),
                  pl.BlockSpec((1, seq, MLA_QK_PAD), lambda b, h: (b, 0, h)),
                  pl.BlockSpec((1, nq, MLA_V, tq), lambda b, h: (b, 0, h, 0))],
        out_specs=pl.BlockSpec((1, seq, MLA_V), lambda b, h: (b, 0, h)),
        out_shape=jax.ShapeDtypeStruct((bsz, seq, MLA_OUT), BF16),
        scratch=[pltpu.VMEM((2, tq, tq), F32), pltpu.VMEM((2, tq, tq), BF16)],
        semantics=("parallel", "parallel"))(
            q.reshape(bsz, seq, n), k.reshape(bsz, seq, n), vt)
    return out.reshape(bsz * seq, MLA_OUT)


def _in_proj_weights(w_in):
    seg = lambda i: w_in[:, IN_OFFS[i]:IN_OFFS[i + 1]]
    z, xbc, dt, q_lat, kv_lat, k_rope, ga, gb = (seg(i) for i in range(8))
    main = jnp.concatenate([z, xbc, ga, gb], axis=1).astype(BF16)
    pad = jnp.zeros((w_in.shape[0], SMALL_DIM - SMALL_DT - SSM_HEADS), w_in.dtype)
    small = jnp.concatenate([q_lat, kv_lat, k_rope, _swap_halves(k_rope), dt, pad], axis=1).astype(BF16)
    return main, small


def kernel(x, c, positions, norm1_w, norm2_w, w_mod, b_mod, w_in, ssm_conv_w, ssm_conv_b, ssm_dt_bias, ssm_a_log, ssm_d, ssm_norm_w, w_ssm_out, mla_q_norm_w, w_q_up, mla_kv_norm_w, w_kv_up, qk_norm_q_w, qk_norm_k_w, w_mla_out, w_mix_out, w_ffn_up, ffn_conv_w, ffn_conv_b, w_ffn_down):
    bsz, seq, d = x.shape
    depth = w_mod.shape[0]
    t = bsz * seq
    mod = _modulation(c, w_mod, b_mod)
    tab = _rope_table(positions)
    x2 = x.reshape(t, d)
    row = lambda v: v.astype(F32).reshape(1, -1)
    for l in range(depth):
        shift1, scale1, gate1, shift2, scale2, gate2 = (m.reshape(bsz, 1, d) for m in jnp.split(mod[l], 6, axis=-1))

        w_main, w_small = _in_proj_weights(w_in[l])
        main = _norm_matmul_conv(x2, row(norm1_w[l]), shift1, scale1, w_main, ssm_conv_w[l].astype(F32),
                                 ssm_conv_b[l].astype(F32), seq=seq, tm=1024, tn=1024, n_out=MAIN_DIM,
                                 conv_col0=MAIN_XBC, val_col0=None, name="in_proj_main")
        small = _norm_matmul(x2, row(norm1_w[l]), shift1, scale1, w_small, seq=seq, tm=1024, tn=SMALL_DIM,
                             out_dtype=F32, name="in_proj_small")
        y_ssd = _ssd(main, small, ssm_dt_bias[l], ssm_a_log[l], ssm_d[l], ssm_norm_w[l], bsz=bsz, seq=seq)
        q, k, vt = _mla_prep(small, tab, mla_q_norm_w[l], mla_kv_norm_w[l], w_q_up[l], w_kv_up[l],
                             qk_norm_q_w[l], qk_norm_k_w[l], bsz=bsz, seq=seq, tm=512)
        o_mla = _attention(q, k, vt, bsz=bsz, seq=seq, tq=512)
        merged = _merge(y_ssd, o_mla, w_ssm_out[l].astype(BF16), w_mla_out[l].astype(BF16), main, tm=512, tn=512)
        x2 = _matmul_residual(merged, w_mix_out[l].astype(BF16), x2, gate1, seq=seq, tm=1024, tn=512,
                              name="mix_out")

        act = _norm_matmul_conv(x2, row(norm2_w[l]), shift2, scale2, w_ffn_up[l].astype(BF16),
                                ffn_conv_w[l].astype(F32), ffn_conv_b[l].astype(F32), seq=seq, tm=1024, tn=512,
                                n_out=FFN_DIM, conv_col0=0, val_col0=FFN_DIM, name="ffn_up_glu")
        x2 = _matmul_residual(act, w_ffn_down[l].astype(BF16), x2, gate2, seq=seq, tm=1024, tn=512,
                              name="ffn_down")
    return x2.reshape(bsz, seq, d)
```

```python
import functools
import math

import jax
import jax.numpy as jnp
import numpy as np
from jax import lax
from jax.experimental import pallas as pl
from jax.experimental.pallas import tpu as pltpu

F32 = jnp.float32
BF16 = jnp.bfloat16

D_MODEL = 2048
CHUNK = 64
EPS = 1e-6

SSM_D_INNER = 2 * D_MODEL
SSM_HEAD_DIM = 64
SSM_HEADS = SSM_D_INNER // SSM_HEAD_DIM
SSM_GROUPS = 8
SSM_HEADS_PER_GROUP = SSM_HEADS // SSM_GROUPS
SSM_STATE = 128
SSM_CONV = 4
SSM_BC = SSM_GROUPS * SSM_STATE
SSM_CONV_DIM = SSM_D_INNER + 2 * SSM_BC
SSM_GROUP_WIDTH = SSM_HEADS_PER_GROUP * SSM_HEAD_DIM

MLA_HEADS = 16
MLA_Q_RANK = 512
MLA_KV_RANK = 512
MLA_NOPE = 128
MLA_ROPE = 64
MLA_V = 128
MLA_QK = MLA_NOPE + MLA_ROPE
MLA_OUT = MLA_HEADS * MLA_V
MLA_QK_PAD = 256
ROPE_BASE = 10000.0
ATTN_SCALE = MLA_QK ** -0.5

FFN_DIM = 5632
FFN_CONV = 3

IN_SIZES = (SSM_D_INNER, SSM_CONV_DIM, SSM_HEADS, MLA_Q_RANK, MLA_KV_RANK, MLA_ROPE, D_MODEL, D_MODEL)
IN_OFFS = tuple(int(v) for v in np.cumsum((0,) + IN_SIZES))

MAIN_Z = 0
MAIN_XBC = SSM_D_INNER
MAIN_GA = MAIN_XBC + SSM_CONV_DIM
MAIN_GB = MAIN_GA + D_MODEL
MAIN_DIM = MAIN_GB + D_MODEL
SMALL_QLAT = 0
SMALL_KVLAT = MLA_Q_RANK
SMALL_ROPE = SMALL_KVLAT + MLA_KV_RANK
SMALL_DT = SMALL_ROPE + 2 * MLA_ROPE
SMALL_DIM = SMALL_DT + 128

LANES = 128
HALO = 16
LOG2E = math.log2(math.e)
VMEM_LIMIT_MB = 56


def _call(kernel, *, name, grid, in_specs, out_specs, out_shape, semantics, scratch=()):
    return pl.pallas_call(
        kernel, grid=grid, in_specs=in_specs, out_specs=out_specs, out_shape=out_shape,
        scratch_shapes=list(scratch), name=name,
        compiler_params=pltpu.CompilerParams(dimension_semantics=semantics,
                                             vmem_limit_bytes=VMEM_LIMIT_MB * 1024 * 1024))


def _silu(v):
    return v * jax.nn.sigmoid(v)


def _rms(v, w):
    ms = jnp.mean(v * v, axis=-1, keepdims=True)
    return v * lax.rsqrt(ms + EPS) * w


def _mod_kernel(c_ref, w_ref, b_ref, o_ref):
    cond = _silu(c_ref[...]).astype(BF16)
    o_ref[0] = jnp.dot(cond, w_ref[0].astype(BF16), preferred_element_type=F32) + b_ref[0]


def _modulation(c, w_mod, b_mod):
    depth, d, n = w_mod.shape
    bsz = c.shape[0]
    tn = 1024
    c8 = jnp.zeros((8, d), F32).at[:bsz].set(c)
    out = _call(
        _mod_kernel, name="adaln_mod", grid=(depth, n // tn),
        in_specs=[pl.BlockSpec((8, d), lambda l, j: (0, 0)),
                  pl.BlockSpec((1, d, tn), lambda l, j: (l, 0, j)),
                  pl.BlockSpec((1, 1, tn), lambda l, j: (l, 0, j))],
        out_specs=pl.BlockSpec((1, 8, tn), lambda l, j: (l, 0, j)),
        out_shape=jax.ShapeDtypeStruct((depth, 8, n), F32),
        semantics=("parallel", "parallel"))(c8, w_mod, b_mod.reshape(depth, 1, n))
    return out[:, :bsz]


def _rope_kernel(pos_ref, freq_ref, o_ref):
    ang = pos_ref[...].astype(F32) * freq_ref[...]
    lane = lax.broadcasted_iota(jnp.int32, ang.shape, 1)
    cos = jnp.cos(ang)
    sin = jnp.sin(ang)
    o_ref[...] = jnp.where(lane < 64, cos, jnp.where(lane < 96, -sin, sin))


def _rope_table(positions):
    t = positions.size
    rows = 2048
    half = MLA_ROPE // 2
    inv_freq = (ROPE_BASE ** (-np.arange(0, MLA_ROPE, 2, dtype=np.float32) / MLA_ROPE)).astype(np.float32)
    freq = jnp.asarray(np.tile(inv_freq, LANES // half)[None, :])
    return _call(
        _rope_kernel, name="rope_table", grid=(t // rows,),
        in_specs=[pl.BlockSpec((rows, 1), lambda i: (i, 0)),
                  pl.BlockSpec((1, LANES), lambda i: (0, 0))],
        out_specs=pl.BlockSpec((rows, LANES), lambda i: (i, 0)),
        out_shape=jax.ShapeDtypeStruct((t, LANES), F32),
        semantics=("parallel",))(positions.reshape(t, 1), freq)


def _norm_mm_kernel(x_ref, nw_ref, sh_ref, sc_ref, w_ref, o_ref, h_ref, *, tm, rows):
    @pl.when(pl.program_id(1) == 0)
    def _():
        nw = nw_ref[...]
        sc = 1.0 + sc_ref[0]
        sh = sh_ref[0]

        def body(r, carry):
            sl = pl.ds(pl.multiple_of(r * rows, rows), rows)
            h_ref[sl, :] = (_rms(x_ref[sl, :], nw) * sc + sh).astype(BF16)
            return carry

        lax.fori_loop(0, tm // rows, body, 0)

    o_ref[...] = jnp.dot(h_ref[...], w_ref[...], preferred_element_type=F32).astype(o_ref.dtype)


def _norm_matmul(x2, nw, shift, scale, w, *, seq, tm, tn, out_dtype, name):
    t, d = x2.shape
    n = w.shape[1]
    tpb = seq // tm
    kernel = functools.partial(_norm_mm_kernel, tm=tm, rows=256)
    return _call(
        kernel, name=name, grid=(t // tm, n // tn),
        in_specs=[pl.BlockSpec((tm, d), lambda i, j: (i, 0)),
                  pl.BlockSpec((1, d), lambda i, j: (0, 0)),
                  pl.BlockSpec((1, 1, d), lambda i, j: (i // tpb, 0, 0)),
                  pl.BlockSpec((1, 1, d), lambda i, j: (i // tpb, 0, 0)),
                  pl.BlockSpec((d, tn), lambda i, j: (0, j))],
        out_specs=pl.BlockSpec((tm, tn), lambda i, j: (i, j)),
        out_shape=jax.ShapeDtypeStruct((t, n), out_dtype),
        scratch=[pltpu.VMEM((tm, d), BF16)],
        semantics=("parallel", "arbitrary"))(x2, nw, shift, scale, w)


def _mm_resid_kernel(a_ref, w_ref, r_ref, g_ref, o_ref):
    acc = jnp.dot(a_ref[...], w_ref[...], preferred_element_type=F32)
    o_ref[...] = r_ref[...] + g_ref[0] * acc


def _matmul_residual(a, w, res, gate, *, seq, tm, tn, name):
    t, k = a.shape
    n = w.shape[1]
    tpb = seq // tm
    return _call(
        _mm_resid_kernel, name=name, grid=(t // tm, n // tn),
        in_specs=[pl.BlockSpec((tm, k), lambda i, j: (i, 0)),
                  pl.BlockSpec((k, tn), lambda i, j: (0, j)),
                  pl.BlockSpec((tm, tn), lambda i, j: (i, j)),
                  pl.BlockSpec((1, 1, tn), lambda i, j: (i // tpb, 0, j))],
        out_specs=pl.BlockSpec((tm, tn), lambda i, j: (i, j)),
        out_shape=jax.ShapeDtypeStruct((t, n), F32),
        semantics=("parallel", "parallel"))(a, w, res, gate)


def _merge_kernel(y_ref, o_ref, ws_ref, wm_ref, ga_ref, gb_ref, out_ref):
    ys = jnp.dot(y_ref[...], ws_ref[...], preferred_element_type=F32)
    o = jnp.concatenate([o_ref[0, h] for h in range(o_ref.shape[1])], axis=1)
    ym = jnp.dot(o, wm_ref[...], preferred_element_type=F32)
    ga = jax.nn.sigmoid(ga_ref[...].astype(F32))
    gb = jax.nn.sigmoid(gb_ref[...].astype(F32))
    out_ref[...] = (ga * ys + gb * ym).astype(out_ref.dtype)


def _merge(y_ssd, o_mla, w_ssm_out, w_mla_out, main, *, tm, tn):
    t = y_ssd.shape[0]
    n = w_ssm_out.shape[1]
    _, heads, seq, dv = o_mla.shape
    tpb = seq // tm
    ga_blk = MAIN_GA // tn
    gb_blk = MAIN_GB // tn
    return _call(
        _merge_kernel, name="branch_merge", grid=(t // tm, n // tn),
        in_specs=[pl.BlockSpec((tm, y_ssd.shape[1]), lambda i, j: (i, 0)),
                  pl.BlockSpec((1, heads, tm, dv), lambda i, j: (i // tpb, 0, i % tpb, 0)),
                  pl.BlockSpec((w_ssm_out.shape[0], tn), lambda i, j: (0, j)),
                  pl.BlockSpec((w_mla_out.shape[0], tn), lambda i, j: (0, j)),
                  pl.BlockSpec((tm, tn), lambda i, j: (i, ga_blk + j)),
                  pl.BlockSpec((tm, tn), lambda i, j: (i, gb_blk + j))],
        out_specs=pl.BlockSpec((tm, tn), lambda i, j: (i, j)),
        out_shape=jax.ShapeDtypeStruct((t, n), BF16),
        semantics=("parallel", "parallel"))(y_ssd, o_mla, w_ssm_out, w_mla_out, main, main)


def _norm_mm_conv_kernel(x_ref, xh_ref, nw_ref, sh_ref, sc_ref, w_ref, *rest, tm, rows, taps, silu_hi, conv_lo,
                         conv_hi, glu, tiles_per_seq):
    if glu:
        wv_ref, cw_ref, cb_ref, o_ref, h_ref, gbuf = rest
    else:
        cw_ref, cb_ref, o_ref, h_ref, gbuf = rest
    i = pl.program_id(0)
    j = pl.program_id(1)

    @pl.when(j == 0)
    def _():
        nw = nw_ref[...]
        sc = 1.0 + sc_ref[0]
        sh = sh_ref[0]
        h_ref[0:HALO, :] = (_rms(xh_ref[...], nw) * sc + sh).astype(BF16)

        def body(r, carry):
            src = pl.ds(pl.multiple_of(r * rows, rows), rows)
            dst = pl.ds(pl.multiple_of(HALO + r * rows, HALO), rows)
            h_ref[dst, :] = (_rms(x_ref[src, :], nw) * sc + sh).astype(BF16)
            return carry

        lax.fori_loop(0, tm // rows, body, 0)

    def plain(act):
        r = jnp.dot(h_ref[HALO:, :], w_ref[...], preferred_element_type=F32)
        o_ref[...] = (_silu(r) if act else r).astype(o_ref.dtype)

    def conv():
        at_seq_start = (i % tiles_per_seq) == 0
        cw = cw_ref[...]
        cbias = cb_ref[...]
        w = w_ref[...]
        g_halo = jnp.dot(h_ref[0:HALO, :], w, preferred_element_type=F32)
        g = jnp.dot(h_ref[HALO:, :], w, preferred_element_type=F32)
        gbuf[0:HALO, :] = jnp.where(at_seq_start, 0.0, g_halo)
        gbuf[HALO:, :] = g
        acc = cbias + cw[taps - 1:taps, :] * g
        for k in range(taps - 1):
            acc = acc + cw[k:k + 1, :] * gbuf[pl.ds(HALO - (taps - 1) + k, tm), :]
        y = _silu(acc)
        if glu:
            y = y * jnp.dot(h_ref[HALO:, :], wv_ref[...], preferred_element_type=F32)
        o_ref[...] = y.astype(o_ref.dtype)

    if glu:
        conv()
    else:
        pl.when(j < silu_hi)(functools.partial(plain, True))
        pl.when(jnp.logical_and(j >= conv_lo, j < conv_hi))(conv)
        pl.when(j >= conv_hi)(functools.partial(plain, False))


def _norm_matmul_conv(x2, nw, shift, scale, w, conv_w, conv_b, *, seq, tm, tn, n_out, silu_cols, conv_col0, val_col0,
                      name):
    t, d = x2.shape
    taps, cwidth = conv_w.shape
    glu = val_col0 is not None
    tpb = seq // tm
    hb = tm // HALO
    conv_lo = conv_col0 // tn
    n_conv = cwidth // tn
    assert glu or silu_cols == conv_col0
    kernel = functools.partial(_norm_mm_conv_kernel, tm=tm, rows=256, taps=taps, silu_hi=silu_cols // tn,
                               conv_lo=conv_lo, conv_hi=conv_lo + n_conv, glu=glu, tiles_per_seq=tpb)
    conv_idx = lambda i, j: (0, jnp.clip(j - conv_lo, 0, n_conv - 1))
    in_specs = [pl.BlockSpec((tm, d), lambda i, j: (i, 0)),
                pl.BlockSpec((HALO, d), lambda i, j: (jnp.maximum(i * hb - 1, 0), 0)),
                pl.BlockSpec((1, d), lambda i, j: (0, 0)),
                pl.BlockSpec((1, 1, d), lambda i, j: (i // tpb, 0, 0)),
                pl.BlockSpec((1, 1, d), lambda i, j: (i // tpb, 0, 0)),
                pl.BlockSpec((d, tn), lambda i, j: (0, j))]
    args = [x2, x2, nw, shift, scale, w]
    if glu:
        vb = val_col0 // tn
        in_specs.append(pl.BlockSpec((d, tn), lambda i, j: (0, vb + j)))
        args.append(w)
    in_specs += [pl.BlockSpec((taps, tn), conv_idx), pl.BlockSpec((1, tn), conv_idx)]
    args += [conv_w, conv_b.reshape(1, cwidth)]
    scratch = [pltpu.VMEM((tm + HALO, d), BF16), pltpu.VMEM((tm + HALO, tn), F32)]
    return _call(
        kernel, name=name, grid=(t // tm, n_out // tn),
        in_specs=in_specs,
        out_specs=pl.BlockSpec((tm, tn), lambda i, j: (i, j)),
        out_shape=jax.ShapeDtypeStruct((t, n_out), BF16),
        scratch=scratch,
        semantics=("parallel", "arbitrary"))(*args)


def _split3(v):
    hi = v.astype(BF16).astype(F32)
    r1 = v - hi
    mid = r1.astype(BF16).astype(F32)
    lo = (r1 - mid).astype(BF16).astype(F32)
    return hi, mid, lo


def _pack3(v, lane):
    hi, mid, lo = _split3(v)
    first = lane < CHUNK
    p1 = jnp.where(first, hi, pltpu.roll(mid, CHUNK, axis=1))
    p2 = jnp.where(first, lo, 0.0)
    return jnp.concatenate([p1, p2], axis=1).astype(BF16)


def _ssd_kernel(xa_ref, b_ref, c_ref, z_ref, dt_ref, dtb_ref, alog_ref, dskip_ref, nw_ref, tri_ref, rep_ref,
                o_ref, state, y_scr):
    q = CHUNK
    gw = SSM_GROUP_WIDTH

    @pl.when(pl.program_id(1) == 0)
    def _():
        state[...] = jnp.zeros_like(state)

    lane = lax.broadcasted_iota(jnp.int32, (q, LANES), 1)
    row = lax.broadcasted_iota(jnp.int32, (q, gw), 0)
    col = lax.broadcasted_iota(jnp.int32, (q, gw), 1) % q
    causal_bias = jnp.where(col <= row, 0.0, -jnp.inf)
    diag = col == row
    bd_r = lax.broadcasted_iota(jnp.int32, (4 * q, 4 * q), 0) // q
    bd_c = lax.broadcasted_iota(jnp.int32, (4 * q, 4 * q), 1) // q
    blockdiag = bd_r == bd_c

    for ci in range(o_ref.shape[0] // q):
        _ssd_chunk(slice(ci * q, (ci + 1) * q), y_scr.at[ci], lane, causal_bias, diag, blockdiag,
                   xa_ref, b_ref, c_ref, z_ref, dt_ref, dtb_ref, alog_ref, dskip_ref, nw_ref, tri_ref, rep_ref,
                   o_ref, state)


def _ssd_chunk(rs, y_scr, lane, causal_bias, diag, blockdiag,
               xa_ref, b_ref, c_ref, z_ref, dt_ref, dtb_ref, alog_ref, dskip_ref, nw_ref, tri_ref, rep_ref,
               o_ref, state):
    q = CHUNK
    gw = SSM_GROUP_WIDTH
    x_dt = dt_ref[rs, :] + dtb_ref[...]
    dt = jnp.maximum(x_dt, 0.0) + jnp.log1p(jnp.exp(-jnp.abs(x_dt)))
    dta = dt * (-jnp.exp(alog_ref[...])) * LOG2E

    hi, mid, lo = _split3(dta)
    stacked = jnp.concatenate([hi, mid, lo, jnp.zeros_like(hi)], axis=0).astype(BF16)
    acs = jnp.dot(tri_ref[...], stacked, preferred_element_type=F32)

    lhs = jnp.concatenate([_pack3(acs, lane), _pack3(dt, lane)], axis=0)

    def matmuls_on_inputs(g):
        gs = slice(g * gw, (g + 1) * gw)
        ns = slice(g * SSM_STATE, (g + 1) * SSM_STATE)
        rg = jnp.dot(lhs, rep_ref[:, gs], preferred_element_type=F32)
        bg = b_ref[rs, ns]
        cg = c_ref[rs, ns]
        b_tiled = jnp.concatenate([bg] * SSM_HEADS_PER_GROUP, axis=0)
        cb = lax.dot_general(cg, b_tiled, (((1,), (1,)), ((), ())), preferred_element_type=F32)
        st = state[g]
        y_off = jnp.dot(cg, st.astype(BF16), preferred_element_type=F32)
        return rg, bg, cb, st, y_off

    def finish_group(g, rg, bg, cb, st, y_off):
        gs = slice(g * gw, (g + 1) * gw)
        colb = rg[:q]
        dtrep = rg[q:]
        acs_row = jnp.sum(jnp.where(diag, colb, 0.0), axis=0, keepdims=True)
        m = (cb * jnp.exp2(colb - acs_row + causal_bias)).astype(BF16)
        xf = xa_ref[rs, gs].astype(F32)
        xdt = xf * dtrep
        xdt_b = xdt.astype(BF16)
        y_parts = []
        for quad in range(2):
            xs = xdt_b[:, quad * 4 * q:(quad + 1) * 4 * q]
            x_bd = jnp.where(blockdiag, jnp.concatenate([xs] * 4, axis=0), jnp.zeros((), BF16))
            y_parts.append(jnp.dot(m[:, quad * 4 * q:(quad + 1) * 4 * q], x_bd, preferred_element_type=F32))
        y_diag = jnp.concatenate(y_parts, axis=1)
        last = colb[q - 1:q, :]
        xd = (xdt * jnp.exp2(last - colb)).astype(BF16)
        state[g] = st * jnp.exp2(last) + lax.dot_general(bg, xd, (((0,), (0,)), ((), ())),
                                                         preferred_element_type=F32)
        y = y_diag + y_off * jnp.exp2(colb) + xf * dskip_ref[:, gs]
        gated = y * z_ref[rs, gs].astype(F32)
        y_scr[:, gs] = gated
        sq = gated * gated
        return sum(sq[:, k * LANES:(k + 1) * LANES] for k in range(gw // LANES))

    ahead = matmuls_on_inputs(0)
    sumsq = jnp.zeros((q, LANES), F32)
    for g in range(SSM_GROUPS):
        current = ahead
        if g + 1 < SSM_GROUPS:
            ahead = matmuls_on_inputs(g + 1)
        sumsq = sumsq + finish_group(g, *current)

    inv = lax.rsqrt(jnp.sum(sumsq, axis=-1, keepdims=True) / SSM_D_INNER + EPS)
    o_ref[rs, :] = (y_scr[...] * inv * nw_ref[...]).astype(o_ref.dtype)


def _ssd_constants():
    q = CHUNK
    tri = np.tril(np.ones((q, q), np.float32))
    tri4 = np.concatenate([tri, tri, tri, np.zeros_like(tri)], axis=1)
    rep = np.zeros((4 * q, SSM_D_INNER), np.float32)
    for j in range(3):
        for r in range(SSM_HEADS):
            rep[j * q + r, r * SSM_HEAD_DIM:(r + 1) * SSM_HEAD_DIM] = 1.0
    return jnp.asarray(tri4, BF16), jnp.asarray(rep, BF16)


def _ssd(main, small, dt_bias, a_log, d_skip, norm_w, *, bsz, seq, step_chunks):
    t = bsz * seq
    q = CHUNK
    rows = step_chunks * q
    ns = seq // rows
    tri4, rep = _ssd_constants()
    pad = lambda v: jnp.concatenate([v.astype(F32), jnp.zeros((LANES - SSM_HEADS,), F32)]).reshape(1, LANES)
    dskip_row = jnp.repeat(d_skip.astype(F32), SSM_HEAD_DIM).reshape(1, SSM_D_INNER)
    rowmap = lambda b, c: b * ns + c
    return _call(
        _ssd_kernel, name="ssd_scan", grid=(bsz, ns),
        in_specs=[pl.BlockSpec((rows, SSM_D_INNER), lambda b, c: (rowmap(b, c), MAIN_XBC // SSM_D_INNER)),
                  pl.BlockSpec((rows, SSM_BC), lambda b, c: (rowmap(b, c), (MAIN_XBC + SSM_D_INNER) // SSM_BC)),
                  pl.BlockSpec((rows, SSM_BC), lambda b, c: (rowmap(b, c), (MAIN_XBC + SSM_D_INNER) // SSM_BC + 1)),
                  pl.BlockSpec((rows, SSM_D_INNER), lambda b, c: (rowmap(b, c), MAIN_Z // SSM_D_INNER)),
                  pl.BlockSpec((rows, LANES), lambda b, c: (rowmap(b, c), SMALL_DT // LANES)),
                  pl.BlockSpec((1, LANES), lambda b, c: (0, 0)),
                  pl.BlockSpec((1, LANES), lambda b, c: (0, 0)),
                  pl.BlockSpec((1, SSM_D_INNER), lambda b, c: (0, 0)),
                  pl.BlockSpec((1, SSM_D_INNER), lambda b, c: (0, 0)),
                  pl.BlockSpec((q, 4 * q), lambda b, c: (0, 0)),
                  pl.BlockSpec((4 * q, SSM_D_INNER), lambda b, c: (0, 0))],
        out_specs=pl.BlockSpec((rows, SSM_D_INNER), lambda b, c: (rowmap(b, c), 0)),
        out_shape=jax.ShapeDtypeStruct((t, SSM_D_INNER), BF16),
        scratch=[pltpu.VMEM((SSM_GROUPS, SSM_STATE, SSM_GROUP_WIDTH), F32),
                 pltpu.VMEM((step_chunks, q, SSM_D_INNER), F32)],
        semantics=("parallel", "arbitrary"))(
            main, main, main, main, small, pad(dt_bias), pad(a_log), dskip_row,
            norm_w.astype(F32).reshape(1, SSM_D_INNER), tri4, rep)


def _mla_prep_kernel(ql_ref, kvl_ref, rope_ref, tab_ref, qnw_ref, kvnw_ref, wq_ref, wk_ref, wvt_ref, gq_ref, gk_ref,
                     q_ref, k_ref, vt_ref):
    qn = _rms(ql_ref[...], qnw_ref[...]).astype(BF16)
    kvn = _rms(kvl_ref[...], kvnw_ref[...]).astype(BF16)
    q = jnp.dot(qn, wq_ref[...], preferred_element_type=F32)
    kn = jnp.dot(kvn, wk_ref[...], preferred_element_type=F32)
    vt = lax.dot_general(wvt_ref[...], kvn, (((1,), (1,)), ((), ())), preferred_element_type=F32).astype(BF16)
    for h in range(MLA_HEADS):
        vt_ref[0, h, 0] = vt[h * MLA_V:(h + 1) * MLA_V, :]
    tab = tab_ref[...]
    rope = rope_ref[...]
    lane = lax.broadcasted_iota(jnp.int32, rope.shape, 1)
    first = lane < MLA_ROPE
    gq = gq_ref[...]
    gk = gk_ref[...]
    ss_rope = jnp.sum(jnp.where(first, rope * rope, 0.0), axis=-1, keepdims=True)
    kr = rope * (gk[:, MLA_NOPE:] * tab)
    kr = kr + pltpu.roll(kr, MLA_ROPE, axis=1)
    q_rope_mul = gq[:, MLA_NOPE:] * tab * ATTN_SCALE
    q_nope_mul = gq[:, :MLA_NOPE] * ATTN_SCALE
    for h in range(MLA_HEADS):
        c0 = h * MLA_QK_PAD
        qa = q[:, c0:c0 + MLA_NOPE]
        qb = q[:, c0 + MLA_NOPE:c0 + MLA_QK_PAD]
        ss = (jnp.sum(qa * qa, axis=-1, keepdims=True)
              + jnp.sum(jnp.where(first, qb * qb, 0.0), axis=-1, keepdims=True))
        inv = lax.rsqrt(ss / MLA_QK + EPS)
        q_ref[0, h, :, :MLA_NOPE] = (qa * inv * q_nope_mul).astype(BF16)
        q_ref[0, h, :, MLA_NOPE:] = (qb * inv * q_rope_mul).astype(BF16)
        ka = kn[:, h * MLA_NOPE:(h + 1) * MLA_NOPE]
        ssk = jnp.sum(ka * ka, axis=-1, keepdims=True) + ss_rope
        invk = lax.rsqrt(ssk / MLA_QK + EPS)
        k_ref[0, h, :, :MLA_NOPE] = (ka * invk * gk[:, :MLA_NOPE]).astype(BF16)
        k_ref[0, h, :, MLA_NOPE:] = (kr * invk).astype(BF16)


def _swap_halves(v):
    half = v.shape[-1] // 2
    return jnp.concatenate([v[..., half:], v[..., :half]], axis=-1)


def _mla_prep(small, tab, q_norm_w, kv_norm_w, w_q_up, w_kv_up, qn_w, kn_w, *, bsz, seq, tm):
    t = small.shape[0]
    tpb = seq // tm
    wq = w_q_up.reshape(MLA_Q_RANK, MLA_HEADS, MLA_QK)
    wq = jnp.concatenate([wq, _swap_halves(wq[..., MLA_NOPE:])], axis=-1)
    wq = wq.reshape(MLA_Q_RANK, MLA_HEADS * MLA_QK_PAD).astype(BF16)
    wkv = w_kv_up.reshape(MLA_KV_RANK, MLA_HEADS, MLA_NOPE + MLA_V)
    wk = wkv[..., :MLA_NOPE].reshape(MLA_KV_RANK, MLA_HEADS * MLA_NOPE).astype(BF16)
    wvt = wkv[..., MLA_NOPE:].reshape(MLA_KV_RANK, MLA_OUT).T.astype(BF16)
    row = lambda g: jnp.concatenate([g, _swap_halves(g[MLA_NOPE:])]).astype(F32).reshape(1, MLA_QK_PAD)
    n = MLA_HEADS * MLA_QK_PAD
    return _call(
        _mla_prep_kernel, name="mla_prep", grid=(t // tm,),
        in_specs=[pl.BlockSpec((tm, MLA_Q_RANK), lambda i: (i, SMALL_QLAT // MLA_Q_RANK)),
                  pl.BlockSpec((tm, MLA_KV_RANK), lambda i: (i, SMALL_KVLAT // MLA_KV_RANK)),
                  pl.BlockSpec((tm, LANES), lambda i: (i, SMALL_ROPE // LANES)),
                  pl.BlockSpec((tm, LANES), lambda i: (i, 0)),
                  pl.BlockSpec((1, MLA_Q_RANK), lambda i: (0, 0)),
                  pl.BlockSpec((1, MLA_KV_RANK), lambda i: (0, 0)),
                  pl.BlockSpec((MLA_Q_RANK, n), lambda i: (0, 0)),
                  pl.BlockSpec((MLA_KV_RANK, MLA_HEADS * MLA_NOPE), lambda i: (0, 0)),
                  pl.BlockSpec((MLA_OUT, MLA_KV_RANK), lambda i: (0, 0)),
                  pl.BlockSpec((1, MLA_QK_PAD), lambda i: (0, 0)),
                  pl.BlockSpec((1, MLA_QK_PAD), lambda i: (0, 0))],
        out_specs=[pl.BlockSpec((1, MLA_HEADS, tm, MLA_QK_PAD), lambda i: (i // tpb, 0, i % tpb, 0)),
                   pl.BlockSpec((1, MLA_HEADS, tm, MLA_QK_PAD), lambda i: (i // tpb, 0, i % tpb, 0)),
                   pl.BlockSpec((1, MLA_HEADS, 1, MLA_V, tm), lambda i: (i // tpb, 0, i % tpb, 0, 0))],
        out_shape=[jax.ShapeDtypeStruct((bsz, MLA_HEADS, seq, MLA_QK_PAD), BF16),
                   jax.ShapeDtypeStruct((bsz, MLA_HEADS, seq, MLA_QK_PAD), BF16),
                   jax.ShapeDtypeStruct((bsz, MLA_HEADS, tpb, MLA_V, tm), BF16)],
        semantics=("parallel",))(
            small, small, small, tab, q_norm_w.astype(F32).reshape(1, MLA_Q_RANK),
            kv_norm_w.astype(F32).reshape(1, MLA_KV_RANK), wq, wk, wvt, row(qn_w), row(kn_w))


def _attn_kernel(q_ref, k_ref, vt_ref, o_ref, s_buf, p_buf, *, tq, nq):
    n_tiles = nq * (nq + 1) // 2
    nt_dims = (((1,), (1,)), ((), ()))

    def rows(idx):
        return pl.ds(pl.multiple_of(idx * tq, tq), tq)

    def finish(qi_p, kj_p, alpha_p, l_p, acc, slot_p):
        acc = alpha_p * acc + jnp.dot(vt_ref[0, 0, kj_p], p_buf[slot_p], preferred_element_type=F32)
        o_ref[0, 0, rows(qi_p), :] = (acc / l_p).T.astype(o_ref.dtype)
        return acc

    def step(t, carry, *, masked):
        qi, kj, qi_p, kj_p, m, l, alpha_p, l_p, acc = carry
        slot = t % 2
        acc = finish(qi_p, kj_p, alpha_p, l_p, acc, 1 - slot)

        s = s_buf[slot]
        if masked:
            kc = lax.broadcasted_iota(jnp.int32, s.shape, 0) // CHUNK
            qc = lax.broadcasted_iota(jnp.int32, s.shape, 1) // CHUNK
            s = jnp.where(kc <= qc, s, -jnp.inf)
        m = jnp.where(kj == 0, -jnp.inf, m)
        m_new = jnp.maximum(m, jnp.max(s, axis=0, keepdims=True))
        p = jnp.exp(s - m_new)
        alpha = jnp.exp(m - m_new)
        l = alpha * l + jnp.sum(p, axis=0, keepdims=True)
        p_buf[slot] = p.astype(BF16)

        last = kj == qi
        qi_n = jnp.minimum(jnp.where(last, qi + 1, qi), nq - 1)
        kj_n = jnp.where(last, 0, kj + 1)
        s_buf[1 - slot] = lax.dot_general(k_ref[0, 0, rows(kj_n), :], q_ref[0, 0, rows(qi_n), :], nt_dims,
                                          preferred_element_type=F32)
        return qi_n, kj_n, qi, kj, m_new, l, alpha, l, acc

    def body(t, carry):
        return lax.cond(carry[0] == carry[1], functools.partial(step, masked=True),
                        functools.partial(step, masked=False), t, carry)

    s_buf[0] = lax.dot_general(k_ref[0, 0, rows(0), :], q_ref[0, 0, rows(0), :], nt_dims,
                               preferred_element_type=F32)
    p_buf[1] = jnp.zeros(p_buf.shape[1:], BF16)
    zero = jnp.zeros((), jnp.int32)
    init = (zero, zero, zero, zero, jnp.full((1, tq), -jnp.inf, F32), jnp.zeros((1, tq), F32),
            jnp.zeros((1, tq), F32), jnp.ones((1, tq), F32), jnp.zeros((MLA_V, tq), F32))
    _, _, qi_p, kj_p, _, _, alpha_p, l_p, acc = lax.fori_loop(0, n_tiles, body, init)
    finish(qi_p, kj_p, alpha_p, l_p, acc, (n_tiles - 1) % 2)


def _attention(q, k, vt, *, bsz, seq, tq):
    nq = seq // tq
    kernel = functools.partial(_attn_kernel, tq=tq, nq=nq)
    return _call(
        kernel, name="mla_attention", grid=(bsz, MLA_HEADS),
        in_specs=[pl.BlockSpec((1, 1, seq, MLA_QK_PAD), lambda b, h: (b, h, 0, 0)),
                  pl.BlockSpec((1, 1, seq, MLA_QK_PAD), lambda b, h: (b, h, 0, 0)),
                  pl.BlockSpec((1, 1, nq, MLA_V, tq), lambda b, h: (b, h, 0, 0, 0))],
        out_specs=pl.BlockSpec((1, 1, seq, MLA_V), lambda b, h: (b, h, 0, 0)),
        out_shape=jax.ShapeDtypeStruct((bsz, MLA_HEADS, seq, MLA_V), BF16),
        scratch=[pltpu.VMEM((2, tq, tq), F32), pltpu.VMEM((2, tq, tq), BF16)],
        semantics=("parallel", "parallel"))(q, k, vt)


def _in_proj_weights(w_in):
    seg = lambda i: w_in[:, IN_OFFS[i]:IN_OFFS[i + 1]]
    z, xbc, dt, q_lat, kv_lat, k_rope, ga, gb = (seg(i) for i in range(8))
    main = jnp.concatenate([z, xbc, ga, gb], axis=1).astype(BF16)
    pad = jnp.zeros((w_in.shape[0], SMALL_DIM - SMALL_DT - SSM_HEADS), w_in.dtype)
    small = jnp.concatenate([q_lat, kv_lat, k_rope, _swap_halves(k_rope), dt, pad], axis=1).astype(BF16)
    return main, small


def kernel(x, c, positions, norm1_w, norm2_w, w_mod, b_mod, w_in, ssm_conv_w, ssm_conv_b, ssm_dt_bias, ssm_a_log, ssm_d, ssm_norm_w, w_ssm_out, mla_q_norm_w, w_q_up, mla_kv_norm_w, w_kv_up, qk_norm_q_w, qk_norm_k_w, w_mla_out, w_mix_out, w_ffn_up, ffn_conv_w, ffn_conv_b, w_ffn_down):
    bsz, seq, d = x.shape
    depth = w_mod.shape[0]
    t = bsz * seq
    mod = _modulation(c, w_mod, b_mod)
    tab = _rope_table(positions)
    x2 = x.reshape(t, d)
    row = lambda v: v.astype(F32).reshape(1, -1)
    for l in range(depth):
        shift1, scale1, gate1, shift2, scale2, gate2 = (m.reshape(bsz, 1, d) for m in jnp.split(mod[l], 6, axis=-1))

        w_main, w_small = _in_proj_weights(w_in[l])
        main = _norm_matmul_conv(x2, row(norm1_w[l]), shift1, scale1, w_main, ssm_conv_w[l].astype(F32),
                                 ssm_conv_b[l].astype(F32), seq=seq, tm=1024, tn=1024, n_out=MAIN_DIM,
                                 silu_cols=MAIN_XBC, conv_col0=MAIN_XBC, val_col0=None, name="in_proj_main")
        small = _norm_matmul(x2, row(norm1_w[l]), shift1, scale1, w_small, seq=seq, tm=1024, tn=SMALL_DIM,
                             out_dtype=F32, name="in_proj_small")
        y_ssd = _ssd(main, small, ssm_dt_bias[l], ssm_a_log[l], ssm_d[l], ssm_norm_w[l], bsz=bsz, seq=seq,
                     step_chunks=4)
        q, k, vt = _mla_prep(small, tab, mla_q_norm_w[l], mla_kv_norm_w[l], w_q_up[l], w_kv_up[l],
                             qk_norm_q_w[l], qk_norm_k_w[l], bsz=bsz, seq=seq, tm=512)
        o_mla = _attention(q, k, vt, bsz=bsz, seq=seq, tq=512)
        merged = _merge(y_ssd, o_mla, w_ssm_out[l].astype(BF16), w_mla_out[l].astype(BF16), main, tm=512, tn=512)
        x2 = _matmul_residual(merged, w_mix_out[l].astype(BF16), x2, gate1, seq=seq, tm=1024, tn=512,
                              name="mix_out")

        act = _norm_matmul_conv(x2, row(norm2_w[l]), shift2, scale2, w_ffn_up[l].astype(BF16),
                                ffn_conv_w[l].astype(F32), ffn_conv_b[l].astype(F32), seq=seq, tm=1024, tn=512,
                                n_out=FFN_DIM, silu_cols=0, conv_col0=0, val_col0=FFN_DIM, name="ffn_up_glu")
        x2 = _matmul_residual(act, w_ffn_down[l].astype(BF16), x2, gate2, seq=seq, tm=1024, tn=512,
                              name="ffn_down")
    return x2.reshape(bsz, seq, d)
```

```python
import functools
import math

import jax
import jax.numpy as jnp
import numpy as np
from jax import lax
from jax.experimental import pallas as pl
from jax.experimental.pallas import tpu as pltpu

F32 = jnp.float32
BF16 = jnp.bfloat16

D_MODEL = 2048
CHUNK = 64
EPS = 1e-6

SSM_D_INNER = 2 * D_MODEL
SSM_HEAD_DIM = 64
SSM_HEADS = SSM_D_INNER // SSM_HEAD_DIM
SSM_GROUPS = 8
SSM_HEADS_PER_GROUP = SSM_HEADS // SSM_GROUPS
SSM_STATE = 128
SSM_CONV = 4
SSM_BC = SSM_GROUPS * SSM_STATE
SSM_CONV_DIM = SSM_D_INNER + 2 * SSM_BC
SSM_GROUP_WIDTH = SSM_HEADS_PER_GROUP * SSM_HEAD_DIM

MLA_HEADS = 16
MLA_Q_RANK = 512
MLA_KV_RANK = 512
MLA_NOPE = 128
MLA_ROPE = 64
MLA_V = 128
MLA_QK = MLA_NOPE + MLA_ROPE
MLA_OUT = MLA_HEADS * MLA_V
MLA_QK_PAD = 256
ROPE_BASE = 10000.0
ATTN_SCALE = MLA_QK ** -0.5

FFN_DIM = 5632
FFN_CONV = 3

IN_SIZES = (SSM_D_INNER, SSM_CONV_DIM, SSM_HEADS, MLA_Q_RANK, MLA_KV_RANK, MLA_ROPE, D_MODEL, D_MODEL)
IN_OFFS = tuple(int(v) for v in np.cumsum((0,) + IN_SIZES))

MAIN_Z = 0
MAIN_XBC = SSM_D_INNER
MAIN_GA = MAIN_XBC + SSM_CONV_DIM
MAIN_GB = MAIN_GA + D_MODEL
MAIN_DIM = MAIN_GB + D_MODEL
SMALL_QLAT = 0
SMALL_KVLAT = MLA_Q_RANK
SMALL_ROPE = SMALL_KVLAT + MLA_KV_RANK
SMALL_DT = SMALL_ROPE + 2 * MLA_ROPE
SMALL_DIM = SMALL_DT + 128

LANES = 128
HALO = 16
LOG2E = math.log2(math.e)
VMEM_LIMIT_MB = 56


def _call(kernel, *, name, grid, in_specs, out_specs, out_shape, semantics, scratch=()):
    return pl.pallas_call(
        kernel, grid=grid, in_specs=in_specs, out_specs=out_specs, out_shape=out_shape,
        scratch_shapes=list(scratch), name=name,
        compiler_params=pltpu.CompilerParams(dimension_semantics=semantics,
                                             vmem_limit_bytes=VMEM_LIMIT_MB * 1024 * 1024))


def _silu(v):
    return v * jax.nn.sigmoid(v)


def _rms(v, w):
    ms = jnp.mean(v * v, axis=-1, keepdims=True)
    return v * lax.rsqrt(ms + EPS) * w


def _mod_kernel(c_ref, w_ref, b_ref, o_ref):
    cond = _silu(c_ref[...]).astype(BF16)
    o_ref[0] = jnp.dot(cond, w_ref[0].astype(BF16), preferred_element_type=F32) + b_ref[0]


def _modulation(c, w_mod, b_mod):
    depth, d, n = w_mod.shape
    bsz = c.shape[0]
    tn = 1024
    c8 = jnp.zeros((8, d), F32).at[:bsz].set(c)
    out = _call(
        _mod_kernel, name="adaln_mod", grid=(depth, n // tn),
        in_specs=[pl.BlockSpec((8, d), lambda l, j: (0, 0)),
                  pl.BlockSpec((1, d, tn), lambda l, j: (l, 0, j)),
                  pl.BlockSpec((1, 1, tn), lambda l, j: (l, 0, j))],
        out_specs=pl.BlockSpec((1, 8, tn), lambda l, j: (l, 0, j)),
        out_shape=jax.ShapeDtypeStruct((depth, 8, n), F32),
        semantics=("parallel", "parallel"))(c8, w_mod, b_mod.reshape(depth, 1, n))
    return out[:, :bsz]


def _rope_kernel(pos_ref, freq_ref, o_ref):
    ang = pos_ref[...].astype(F32) * freq_ref[...]
    lane = lax.broadcasted_iota(jnp.int32, ang.shape, 1)
    cos = jnp.cos(ang)
    sin = jnp.sin(ang)
    o_ref[...] = jnp.where(lane < 64, cos, jnp.where(lane < 96, -sin, sin))


def _rope_table(positions):
    t = positions.size
    rows = 2048
    half = MLA_ROPE // 2
    inv_freq = (ROPE_BASE ** (-np.arange(0, MLA_ROPE, 2, dtype=np.float32) / MLA_ROPE)).astype(np.float32)
    freq = jnp.asarray(np.tile(inv_freq, LANES // half)[None, :])
    return _call(
        _rope_kernel, name="rope_table", grid=(t // rows,),
        in_specs=[pl.BlockSpec((rows, 1), lambda i: (i, 0)),
                  pl.BlockSpec((1, LANES), lambda i: (0, 0))],
        out_specs=pl.BlockSpec((rows, LANES), lambda i: (i, 0)),
        out_shape=jax.ShapeDtypeStruct((t, LANES), F32),
        semantics=("parallel",))(positions.reshape(t, 1), freq)


def _norm_mm_kernel(x_ref, nw_ref, sh_ref, sc_ref, w_ref, o_ref, h_ref, *, tm, rows):
    @pl.when(pl.program_id(1) == 0)
    def _():
        nw = nw_ref[...]
        sc = 1.0 + sc_ref[0]
        sh = sh_ref[0]

        def body(r, carry):
            sl = pl.ds(pl.multiple_of(r * rows, rows), rows)
            h_ref[sl, :] = (_rms(x_ref[sl, :], nw) * sc + sh).astype(BF16)
            return carry

        lax.fori_loop(0, tm // rows, body, 0)

    o_ref[...] = jnp.dot(h_ref[...], w_ref[0], preferred_element_type=F32).astype(o_ref.dtype)


def _norm_matmul(x2, nw, shift, scale, w, layer, *, seq, tm, tn, out_dtype, name):
    t, d = x2.shape
    n = w.shape[2]
    tpb = seq // tm
    kernel = functools.partial(_norm_mm_kernel, tm=tm, rows=256)
    return _call(
        kernel, name=name, grid=(t // tm, n // tn),
        in_specs=[pl.BlockSpec((tm, d), lambda i, j: (i, 0)),
                  pl.BlockSpec((1, d), lambda i, j: (0, 0)),
                  pl.BlockSpec((1, 1, d), lambda i, j: (i // tpb, 0, 0)),
                  pl.BlockSpec((1, 1, d), lambda i, j: (i // tpb, 0, 0)),
                  pl.BlockSpec((1, d, tn), lambda i, j: (layer, 0, j))],
        out_specs=pl.BlockSpec((tm, tn), lambda i, j: (i, j)),
        out_shape=jax.ShapeDtypeStruct((t, n), out_dtype),
        scratch=[pltpu.VMEM((tm, d), BF16)],
        semantics=("parallel", "arbitrary"))(x2, nw, shift, scale, w)


def _mm_resid_kernel(a_ref, w_ref, r_ref, g_ref, o_ref):
    acc = jnp.dot(a_ref[...], w_ref[0], preferred_element_type=F32)
    o_ref[...] = r_ref[...] + g_ref[0] * acc


def _matmul_residual(a, w, layer, res, gate, *, seq, tm, tn, name):
    t, k = a.shape
    n = w.shape[2]
    tpb = seq // tm
    return _call(
        _mm_resid_kernel, name=name, grid=(t // tm, n // tn),
        in_specs=[pl.BlockSpec((tm, k), lambda i, j: (i, 0)),
                  pl.BlockSpec((1, k, tn), lambda i, j: (layer, 0, j)),
                  pl.BlockSpec((tm, tn), lambda i, j: (i, j)),
                  pl.BlockSpec((1, 1, tn), lambda i, j: (i // tpb, 0, j))],
        out_specs=pl.BlockSpec((tm, tn), lambda i, j: (i, j)),
        out_shape=jax.ShapeDtypeStruct((t, n), F32),
        semantics=("parallel", "parallel"))(a, w, res, gate)


def _merge_kernel(y_ref, o_ref, ws_ref, wm_ref, ga_ref, gb_ref, out_ref):
    ys = jnp.dot(y_ref[...], ws_ref[0], preferred_element_type=F32)
    o = jnp.concatenate([o_ref[0, h] for h in range(o_ref.shape[1])], axis=1)
    ym = jnp.dot(o, wm_ref[0], preferred_element_type=F32)
    ga = jax.nn.sigmoid(ga_ref[...].astype(F32))
    gb = jax.nn.sigmoid(gb_ref[...].astype(F32))
    out_ref[...] = (ga * ys + gb * ym).astype(out_ref.dtype)


def _merge(y_ssd, o_mla, w_ssm_out, w_mla_out, layer, main, *, tm, tn):
    t = y_ssd.shape[0]
    n = w_ssm_out.shape[2]
    _, heads, seq, dv = o_mla.shape
    tpb = seq // tm
    ga_blk = MAIN_GA // tn
    gb_blk = MAIN_GB // tn
    return _call(
        _merge_kernel, name="branch_merge", grid=(t // tm, n // tn),
        in_specs=[pl.BlockSpec((tm, y_ssd.shape[1]), lambda i, j: (i, 0)),
                  pl.BlockSpec((1, heads, tm, dv), lambda i, j: (i // tpb, 0, i % tpb, 0)),
                  pl.BlockSpec((1, w_ssm_out.shape[1], tn), lambda i, j: (layer, 0, j)),
                  pl.BlockSpec((1, w_mla_out.shape[1], tn), lambda i, j: (layer, 0, j)),
                  pl.BlockSpec((tm, tn), lambda i, j: (i, ga_blk + j)),
                  pl.BlockSpec((tm, tn), lambda i, j: (i, gb_blk + j))],
        out_specs=pl.BlockSpec((tm, tn), lambda i, j: (i, j)),
        out_shape=jax.ShapeDtypeStruct((t, n), BF16),
        semantics=("parallel", "parallel"))(y_ssd, o_mla, w_ssm_out, w_mla_out, main, main)


def _norm_mm_conv_kernel(x_ref, xh_ref, nw_ref, sh_ref, sc_ref, w_ref, w2_ref, cw_ref, cb_ref, o_ref, h_ref, gbuf, *,
                         tm, rows, taps, silu_hi, conv_lo, conv_hi, glu, tiles_per_seq):
    i = pl.program_id(0)
    j = pl.program_id(1)

    @pl.when(j == 0)
    def _():
        nw = nw_ref[...]
        sc = 1.0 + sc_ref[0]
        sh = sh_ref[0]
        h_ref[0:HALO, :] = (_rms(xh_ref[...], nw) * sc + sh).astype(BF16)

        def body(r, carry):
            src = pl.ds(pl.multiple_of(r * rows, rows), rows)
            dst = pl.ds(pl.multiple_of(HALO + r * rows, HALO), rows)
            h_ref[dst, :] = (_rms(x_ref[src, :], nw) * sc + sh).astype(BF16)
            return carry

        lax.fori_loop(0, tm // rows, body, 0)

    def plain(act, wref):
        r = jnp.dot(h_ref[HALO:, :], wref[0], preferred_element_type=F32)
        o_ref[...] = (_silu(r) if act else r).astype(o_ref.dtype)

    def conv():
        at_seq_start = (i % tiles_per_seq) == 0
        cw = cw_ref[...]
        cbias = cb_ref[...]
        w = w_ref[0]
        g_halo = jnp.dot(h_ref[0:HALO, :], w, preferred_element_type=F32)
        g = jnp.dot(h_ref[HALO:, :], w, preferred_element_type=F32)
        gbuf[0:HALO, :] = jnp.where(at_seq_start, 0.0, g_halo)
        gbuf[HALO:, :] = g
        acc = cbias + cw[taps - 1:taps, :] * g
        for k in range(taps - 1):
            acc = acc + cw[k:k + 1, :] * gbuf[pl.ds(HALO - (taps - 1) + k, tm), :]
        y = _silu(acc)
        if glu:
            y = y * jnp.dot(h_ref[HALO:, :], w2_ref[0], preferred_element_type=F32)
        o_ref[...] = y.astype(o_ref.dtype)

    if glu:
        conv()
    else:
        pl.when(j < silu_hi)(functools.partial(plain, True, w_ref))
        pl.when(jnp.logical_and(j >= conv_lo, j < conv_hi))(conv)
        pl.when(j >= conv_hi)(functools.partial(plain, False, w2_ref))


def _norm_matmul_conv(x2, nw, shift, scale, w, w2, layer, conv_w, conv_b, *, seq, tm, tn, n_out, silu_cols, conv_col0,
                      val_col0, name):
    t, d = x2.shape
    taps, cwidth = conv_w.shape
    glu = val_col0 is not None
    tpb = seq // tm
    hb = tm // HALO
    conv_lo = conv_col0 // tn
    n_conv = cwidth // tn
    conv_hi = conv_lo + n_conv
    assert glu or silu_cols == conv_col0
    kernel = functools.partial(_norm_mm_conv_kernel, tm=tm, rows=256, taps=taps, silu_hi=silu_cols // tn,
                               conv_lo=conv_lo, conv_hi=conv_hi, glu=glu, tiles_per_seq=tpb)
    conv_idx = lambda i, j: (0, jnp.clip(j - conv_lo, 0, n_conv - 1))
    if glu:
        vb = val_col0 // tn
        w_idx = lambda i, j: (layer, 0, j)
        w2_idx = lambda i, j: (layer, 0, vb + j)
    else:
        w_idx = lambda i, j: (layer, 0, jnp.minimum(j, conv_hi - 1))
        w2_idx = lambda i, j: (layer, 0, jnp.maximum(j - conv_hi, 0))
    return _call(
        kernel, name=name, grid=(t // tm, n_out // tn),
        in_specs=[pl.BlockSpec((tm, d), lambda i, j: (i, 0)),
                  pl.BlockSpec((HALO, d), lambda i, j: (jnp.maximum(i * hb - 1, 0), 0)),
                  pl.BlockSpec((1, d), lambda i, j: (0, 0)),
                  pl.BlockSpec((1, 1, d), lambda i, j: (i // tpb, 0, 0)),
                  pl.BlockSpec((1, 1, d), lambda i, j: (i // tpb, 0, 0)),
                  pl.BlockSpec((1, d, tn), w_idx),
                  pl.BlockSpec((1, d, tn), w2_idx),
                  pl.BlockSpec((taps, tn), conv_idx),
                  pl.BlockSpec((1, tn), conv_idx)],
        out_specs=pl.BlockSpec((tm, tn), lambda i, j: (i, j)),
        out_shape=jax.ShapeDtypeStruct((t, n_out), BF16),
        scratch=[pltpu.VMEM((tm + HALO, d), BF16), pltpu.VMEM((tm + HALO, tn), F32)],
        semantics=("parallel", "arbitrary"))(x2, x2, nw, shift, scale, w, w2, conv_w, conv_b.reshape(1, cwidth))


def _split3(v):
    hi = v.astype(BF16).astype(F32)
    r1 = v - hi
    mid = r1.astype(BF16).astype(F32)
    lo = (r1 - mid).astype(BF16).astype(F32)
    return hi, mid, lo


def _pack3(v, lane):
    hi, mid, lo = _split3(v)
    first = lane < CHUNK
    p1 = jnp.where(first, hi, pltpu.roll(mid, CHUNK, axis=1))
    p2 = jnp.where(first, lo, 0.0)
    return jnp.concatenate([p1, p2], axis=1).astype(BF16)


def _ssd_kernel(xa_ref, b_ref, c_ref, z_ref, dt_ref, dtb_ref, alog_ref, dskip_ref, nw_ref, tri_ref, rep_ref,
                o_ref, state, y_scr):
    q = CHUNK
    gw = SSM_GROUP_WIDTH

    @pl.when(pl.program_id(1) == 0)
    def _():
        state[...] = jnp.zeros_like(state)

    lane = lax.broadcasted_iota(jnp.int32, (q, LANES), 1)
    row = lax.broadcasted_iota(jnp.int32, (q, gw), 0)
    col = lax.broadcasted_iota(jnp.int32, (q, gw), 1) % q
    causal_bias = jnp.where(col <= row, 0.0, -jnp.inf)
    diag = col == row
    bd_r = lax.broadcasted_iota(jnp.int32, (4 * q, 4 * q), 0) // q
    bd_c = lax.broadcasted_iota(jnp.int32, (4 * q, 4 * q), 1) // q
    blockdiag = bd_r == bd_c

    for ci in range(o_ref.shape[0] // q):
        _ssd_chunk(slice(ci * q, (ci + 1) * q), y_scr.at[ci], lane, causal_bias, diag, blockdiag,
                   xa_ref, b_ref, c_ref, z_ref, dt_ref, dtb_ref, alog_ref, dskip_ref, nw_ref, tri_ref, rep_ref,
                   o_ref, state)


def _ssd_chunk(rs, y_scr, lane, causal_bias, diag, blockdiag,
               xa_ref, b_ref, c_ref, z_ref, dt_ref, dtb_ref, alog_ref, dskip_ref, nw_ref, tri_ref, rep_ref,
               o_ref, state):
    q = CHUNK
    gw = SSM_GROUP_WIDTH
    x_dt = dt_ref[rs, :] + dtb_ref[...]
    dt = jnp.maximum(x_dt, 0.0) + jnp.log1p(jnp.exp(-jnp.abs(x_dt)))
    dta = dt * (-jnp.exp(alog_ref[...])) * LOG2E

    hi, mid, lo = _split3(dta)
    stacked = jnp.concatenate([hi, mid, lo, jnp.zeros_like(hi)], axis=0).astype(BF16)
    acs = jnp.dot(tri_ref[...], stacked, preferred_element_type=F32)

    lhs = jnp.concatenate([_pack3(acs, lane), _pack3(dt, lane)], axis=0)

    def matmuls_on_inputs(g):
        gs = slice(g * gw, (g + 1) * gw)
        ns = slice(g * SSM_STATE, (g + 1) * SSM_STATE)
        rg = jnp.dot(lhs, rep_ref[:, gs], preferred_element_type=F32)
        bg = b_ref[rs, ns]
        cg = c_ref[rs, ns]
        b_tiled = jnp.concatenate([bg] * SSM_HEADS_PER_GROUP, axis=0)
        cb = lax.dot_general(cg, b_tiled, (((1,), (1,)), ((), ())), preferred_element_type=F32)
        st = state[g]
        y_off = jnp.dot(cg, st.astype(BF16), preferred_element_type=F32)
        return rg, bg, cb, st, y_off

    def finish_group(g, rg, bg, cb, st, y_off):
        gs = slice(g * gw, (g + 1) * gw)
        colb = rg[:q]
        dtrep = rg[q:]
        acs_row = jnp.sum(jnp.where(diag, colb, 0.0), axis=0, keepdims=True)
        m = (cb * jnp.exp2(colb - acs_row + causal_bias)).astype(BF16)
        xf = xa_ref[rs, gs].astype(F32)
        xdt = xf * dtrep
        xdt_b = xdt.astype(BF16)
        y_parts = []
        for quad in range(2):
            xs = xdt_b[:, quad * 4 * q:(quad + 1) * 4 * q]
            x_bd = jnp.where(blockdiag, jnp.concatenate([xs] * 4, axis=0), jnp.zeros((), BF16))
            y_parts.append(jnp.dot(m[:, quad * 4 * q:(quad + 1) * 4 * q], x_bd, preferred_element_type=F32))
        y_diag = jnp.concatenate(y_parts, axis=1)
        last = colb[q - 1:q, :]
        xd = (xdt * jnp.exp2(last - colb)).astype(BF16)
        state[g] = st * jnp.exp2(last) + lax.dot_general(bg, xd, (((0,), (0,)), ((), ())),
                                                         preferred_element_type=F32)
        y = y_diag + y_off * jnp.exp2(colb) + xf * dskip_ref[:, gs]
        gated = y * z_ref[rs, gs].astype(F32)
        y_scr[:, gs] = gated
        sq = gated * gated
        return sum(sq[:, k * LANES:(k + 1) * LANES] for k in range(gw // LANES))

    ahead = matmuls_on_inputs(0)
    sumsq = jnp.zeros((q, LANES), F32)
    for g in range(SSM_GROUPS):
        current = ahead
        if g + 1 < SSM_GROUPS:
            ahead = matmuls_on_inputs(g + 1)
        sumsq = sumsq + finish_group(g, *current)

    inv = lax.rsqrt(jnp.sum(sumsq, axis=-1, keepdims=True) / SSM_D_INNER + EPS)
    o_ref[rs, :] = (y_scr[...] * inv * nw_ref[...]).astype(o_ref.dtype)


def _ssd_constants():
    q = CHUNK
    tri = np.tril(np.ones((q, q), np.float32))
    tri4 = np.concatenate([tri, tri, tri, np.zeros_like(tri)], axis=1)
    rep = np.zeros((4 * q, SSM_D_INNER), np.float32)
    for j in range(3):
        for r in range(SSM_HEADS):
            rep[j * q + r, r * SSM_HEAD_DIM:(r + 1) * SSM_HEAD_DIM] = 1.0
    return jnp.asarray(tri4, BF16), jnp.asarray(rep, BF16)


def _ssd(main, small, dt_bias, a_log, d_skip, norm_w, *, bsz, seq, step_chunks):
    t = bsz * seq
    q = CHUNK
    rows = step_chunks * q
    ns = seq // rows
    tri4, rep = _ssd_constants()
    pad = lambda v: jnp.concatenate([v.astype(F32), jnp.zeros((LANES - SSM_HEADS,), F32)]).reshape(1, LANES)
    dskip_row = jnp.repeat(d_skip.astype(F32), SSM_HEAD_DIM).reshape(1, SSM_D_INNER)
    rowmap = lambda b, c: b * ns + c
    return _call(
        _ssd_kernel, name="ssd_scan", grid=(bsz, ns),
        in_specs=[pl.BlockSpec((rows, SSM_D_INNER), lambda b, c: (rowmap(b, c), MAIN_XBC // SSM_D_INNER)),
                  pl.BlockSpec((rows, SSM_BC), lambda b, c: (rowmap(b, c), (MAIN_XBC + SSM_D_INNER) // SSM_BC)),
                  pl.BlockSpec((rows, SSM_BC), lambda b, c: (rowmap(b, c), (MAIN_XBC + SSM_D_INNER) // SSM_BC + 1)),
                  pl.BlockSpec((rows, SSM_D_INNER), lambda b, c: (rowmap(b, c), MAIN_Z // SSM_D_INNER)),
                  pl.BlockSpec((rows, LANES), lambda b, c: (rowmap(b, c), SMALL_DT // LANES)),
                  pl.BlockSpec((1, LANES), lambda b, c: (0, 0)),
                  pl.BlockSpec((1, LANES), lambda b, c: (0, 0)),
                  pl.BlockSpec((1, SSM_D_INNER), lambda b, c: (0, 0)),
                  pl.BlockSpec((1, SSM_D_INNER), lambda b, c: (0, 0)),
                  pl.BlockSpec((q, 4 * q), lambda b, c: (0, 0)),
                  pl.BlockSpec((4 * q, SSM_D_INNER), lambda b, c: (0, 0))],
        out_specs=pl.BlockSpec((rows, SSM_D_INNER), lambda b, c: (rowmap(b, c), 0)),
        out_shape=jax.ShapeDtypeStruct((t, SSM_D_INNER), BF16),
        scratch=[pltpu.VMEM((SSM_GROUPS, SSM_STATE, SSM_GROUP_WIDTH), F32),
                 pltpu.VMEM((step_chunks, q, SSM_D_INNER), F32)],
        semantics=("parallel", "arbitrary"))(
            main, main, main, main, small, pad(dt_bias), pad(a_log), dskip_row,
            norm_w.astype(F32).reshape(1, SSM_D_INNER), tri4, rep)


def _mla_prep_kernel(ql_ref, kvl_ref, rope_ref, tab_ref, qnw_ref, kvnw_ref, wq_ref, wk_ref, wvt_ref, gq_ref, gk_ref,
                     q_ref, k_ref, vt_ref):
    qn = _rms(ql_ref[...], qnw_ref[...]).astype(BF16)
    kvn = _rms(kvl_ref[...], kvnw_ref[...]).astype(BF16)
    q = jnp.dot(qn, wq_ref[...], preferred_element_type=F32)
    kn = jnp.dot(kvn, wk_ref[...], preferred_element_type=F32)
    vt = lax.dot_general(wvt_ref[...], kvn, (((1,), (1,)), ((), ())), preferred_element_type=F32).astype(BF16)
    for h in range(MLA_HEADS):
        vt_ref[0, h, 0] = vt[h * MLA_V:(h + 1) * MLA_V, :]
    tab = tab_ref[...]
    rope = rope_ref[...]
    lane = lax.broadcasted_iota(jnp.int32, rope.shape, 1)
    first = lane < MLA_ROPE
    gq = gq_ref[...]
    gk = gk_ref[...]
    ss_rope = jnp.sum(jnp.where(first, rope * rope, 0.0), axis=-1, keepdims=True)
    kr = rope * (gk[:, MLA_NOPE:] * tab)
    kr = kr + pltpu.roll(kr, MLA_ROPE, axis=1)
    q_rope_mul = gq[:, MLA_NOPE:] * tab * ATTN_SCALE
    q_nope_mul = gq[:, :MLA_NOPE] * ATTN_SCALE
    for h in range(MLA_HEADS):
        c0 = h * MLA_QK_PAD
        qa = q[:, c0:c0 + MLA_NOPE]
        qb = q[:, c0 + MLA_NOPE:c0 + MLA_QK_PAD]
        ss = (jnp.sum(qa * qa, axis=-1, keepdims=True)
              + jnp.sum(jnp.where(first, qb * qb, 0.0), axis=-1, keepdims=True))
        inv = lax.rsqrt(ss / MLA_QK + EPS)
        q_ref[0, h, :, :MLA_NOPE] = (qa * inv * q_nope_mul).astype(BF16)
        q_ref[0, h, :, MLA_NOPE:] = (qb * inv * q_rope_mul).astype(BF16)
        ka = kn[:, h * MLA_NOPE:(h + 1) * MLA_NOPE]
        ssk = jnp.sum(ka * ka, axis=-1, keepdims=True) + ss_rope
        invk = lax.rsqrt(ssk / MLA_QK + EPS)
        k_ref[0, h, :, :MLA_NOPE] = (ka * invk * gk[:, :MLA_NOPE]).astype(BF16)
        k_ref[0, h, :, MLA_NOPE:] = (kr * invk).astype(BF16)


def _swap_halves(v):
    half = v.shape[-1] // 2
    return jnp.concatenate([v[..., half:], v[..., :half]], axis=-1)


def _mla_prep(small, tab, q_norm_w, kv_norm_w, w_q_up, w_kv_up, qn_w, kn_w, *, bsz, seq, tm):
    t = small.shape[0]
    tpb = seq // tm
    wq = w_q_up.reshape(MLA_Q_RANK, MLA_HEADS, MLA_QK)
    wq = jnp.concatenate([wq, _swap_halves(wq[..., MLA_NOPE:])], axis=-1)
    wq = wq.reshape(MLA_Q_RANK, MLA_HEADS * MLA_QK_PAD).astype(BF16)
    wkv = w_kv_up.reshape(MLA_KV_RANK, MLA_HEADS, MLA_NOPE + MLA_V)
    wk = wkv[..., :MLA_NOPE].reshape(MLA_KV_RANK, MLA_HEADS * MLA_NOPE).astype(BF16)
    wvt = wkv[..., MLA_NOPE:].reshape(MLA_KV_RANK, MLA_OUT).T.astype(BF16)
    row = lambda g: jnp.concatenate([g, _swap_halves(g[MLA_NOPE:])]).astype(F32).reshape(1, MLA_QK_PAD)
    n = MLA_HEADS * MLA_QK_PAD
    return _call(
        _mla_prep_kernel, name="mla_prep", grid=(t // tm,),
        in_specs=[pl.BlockSpec((tm, MLA_Q_RANK), lambda i: (i, SMALL_QLAT // MLA_Q_RANK)),
                  pl.BlockSpec((tm, MLA_KV_RANK), lambda i: (i, SMALL_KVLAT // MLA_KV_RANK)),
                  pl.BlockSpec((tm, LANES), lambda i: (i, SMALL_ROPE // LANES)),
                  pl.BlockSpec((tm, LANES), lambda i: (i, 0)),
                  pl.BlockSpec((1, MLA_Q_RANK), lambda i: (0, 0)),
                  pl.BlockSpec((1, MLA_KV_RANK), lambda i: (0, 0)),
                  pl.BlockSpec((MLA_Q_RANK, n), lambda i: (0, 0)),
                  pl.BlockSpec((MLA_KV_RANK, MLA_HEADS * MLA_NOPE), lambda i: (0, 0)),
                  pl.BlockSpec((MLA_OUT, MLA_KV_RANK), lambda i: (0, 0)),
                  pl.BlockSpec((1, MLA_QK_PAD), lambda i: (0, 0)),
                  pl.BlockSpec((1, MLA_QK_PAD), lambda i: (0, 0))],
        out_specs=[pl.BlockSpec((1, MLA_HEADS, tm, MLA_QK_PAD), lambda i: (i // tpb, 0, i % tpb, 0)),
                   pl.BlockSpec((1, MLA_HEADS, tm, MLA_QK_PAD), lambda i: (i // tpb, 0, i % tpb, 0)),
                   pl.BlockSpec((1, MLA_HEADS, 1, MLA_V, tm), lambda i: (i // tpb, 0, i % tpb, 0, 0))],
        out_shape=[jax.ShapeDtypeStruct((bsz, MLA_HEADS, seq, MLA_QK_PAD), BF16),
                   jax.ShapeDtypeStruct((bsz, MLA_HEADS, seq, MLA_QK_PAD), BF16),
                   jax.ShapeDtypeStruct((bsz, MLA_HEADS, tpb, MLA_V, tm), BF16)],
        semantics=("parallel",))(
            small, small, small, tab, q_norm_w.astype(F32).reshape(1, MLA_Q_RANK),
            kv_norm_w.astype(F32).reshape(1, MLA_KV_RANK), wq, wk, wvt, row(qn_w), row(kn_w))


def _attn_kernel(q_ref, k_ref, vt_ref, o_ref, s_buf, p_buf, *, tq, nq):
    n_tiles = nq * (nq + 1) // 2
    nt_dims = (((1,), (1,)), ((), ()))

    def rows(idx):
        return pl.ds(pl.multiple_of(idx * tq, tq), tq)

    def finish(qi_p, kj_p, alpha_p, l_p, acc, slot_p):
        acc = alpha_p * acc + jnp.dot(vt_ref[0, 0, kj_p], p_buf[slot_p], preferred_element_type=F32)
        o_ref[0, 0, rows(qi_p), :] = (acc / l_p).T.astype(o_ref.dtype)
        return acc

    def step(t, carry, *, masked):
        qi, kj, qi_p, kj_p, m, l, alpha_p, l_p, acc = carry
        slot = t % 2
        acc = finish(qi_p, kj_p, alpha_p, l_p, acc, 1 - slot)

        s = s_buf[slot]
        if masked:
            kc = lax.broadcasted_iota(jnp.int32, s.shape, 0) // CHUNK
            qc = lax.broadcasted_iota(jnp.int32, s.shape, 1) // CHUNK
            s = jnp.where(kc <= qc, s, -jnp.inf)
        m = jnp.where(kj == 0, -jnp.inf, m)
        m_new = jnp.maximum(m, jnp.max(s, axis=0, keepdims=True))
        p = jnp.exp(s - m_new)
        alpha = jnp.exp(m - m_new)
        l = alpha * l + jnp.sum(p, axis=0, keepdims=True)
        p_buf[slot] = p.astype(BF16)

        last = kj == qi
        qi_n = jnp.minimum(jnp.where(last, qi + 1, qi), nq - 1)
        kj_n = jnp.where(last, 0, kj + 1)
        s_buf[1 - slot] = lax.dot_general(k_ref[0, 0, rows(kj_n), :], q_ref[0, 0, rows(qi_n), :], nt_dims,
                                          preferred_element_type=F32)
        return qi_n, kj_n, qi, kj, m_new, l, alpha, l, acc

    def body(t, carry):
        return lax.cond(carry[0] == carry[1], functools.partial(step, masked=True),
                        functools.partial(step, masked=False), t, carry)

    s_buf[0] = lax.dot_general(k_ref[0, 0, rows(0), :], q_ref[0, 0, rows(0), :], nt_dims,
                               preferred_element_type=F32)
    p_buf[1] = jnp.zeros(p_buf.shape[1:], BF16)
    zero = jnp.zeros((), jnp.int32)
    init = (zero, zero, zero, zero, jnp.full((1, tq), -jnp.inf, F32), jnp.zeros((1, tq), F32),
            jnp.zeros((1, tq), F32), jnp.ones((1, tq), F32), jnp.zeros((MLA_V, tq), F32))
    _, _, qi_p, kj_p, _, _, alpha_p, l_p, acc = lax.fori_loop(0, n_tiles, body, init)
    finish(qi_p, kj_p, alpha_p, l_p, acc, (n_tiles - 1) % 2)


def _attention(q, k, vt, *, bsz, seq, tq):
    nq = seq // tq
    kernel = functools.partial(_attn_kernel, tq=tq, nq=nq)
    return _call(
        kernel, name="mla_attention", grid=(bsz, MLA_HEADS),
        in_specs=[pl.BlockSpec((1, 1, seq, MLA_QK_PAD), lambda b, h: (b, h, 0, 0)),
                  pl.BlockSpec((1, 1, seq, MLA_QK_PAD), lambda b, h: (b, h, 0, 0)),
                  pl.BlockSpec((1, 1, nq, MLA_V, tq), lambda b, h: (b, h, 0, 0, 0))],
        out_specs=pl.BlockSpec((1, 1, seq, MLA_V), lambda b, h: (b, h, 0, 0)),
        out_shape=jax.ShapeDtypeStruct((bsz, MLA_HEADS, seq, MLA_V), BF16),
        scratch=[pltpu.VMEM((2, tq, tq), F32), pltpu.VMEM((2, tq, tq), BF16)],
        semantics=("parallel", "parallel"))(q, k, vt)


def _in_proj_small_weights(w_in_b):
    seg = lambda i: w_in_b[:, :, IN_OFFS[i]:IN_OFFS[i + 1]]
    dt, q_lat, kv_lat, k_rope = seg(2), seg(3), seg(4), seg(5)
    pad = jnp.zeros(w_in_b.shape[:2] + (SMALL_DIM - SMALL_DT - SSM_HEADS,), w_in_b.dtype)
    return jnp.concatenate([q_lat, kv_lat, k_rope, _swap_halves(k_rope), dt, pad], axis=2)


def kernel(x, c, positions, norm1_w, norm2_w, w_mod, b_mod, w_in, ssm_conv_w, ssm_conv_b, ssm_dt_bias, ssm_a_log, ssm_d, ssm_norm_w, w_ssm_out, mla_q_norm_w, w_q_up, mla_kv_norm_w, w_kv_up, qk_norm_q_w, qk_norm_k_w, w_mla_out, w_mix_out, w_ffn_up, ffn_conv_w, ffn_conv_b, w_ffn_down):
    bsz, seq, d = x.shape
    depth = w_mod.shape[0]
    t = bsz * seq
    mod = _modulation(c, w_mod, b_mod)
    tab = _rope_table(positions)
    x2 = x.reshape(t, d)
    row = lambda v: v.astype(F32).reshape(1, -1)
    w_in_b = w_in.astype(BF16)
    w_gates_b = w_in_b[:, :, IN_OFFS[6]:]
    w_small_b = _in_proj_small_weights(w_in_b)
    w_ssm_out_b, w_mla_out_b, w_mix_out_b = w_ssm_out.astype(BF16), w_mla_out.astype(BF16), w_mix_out.astype(BF16)
    w_ffn_up_b, w_ffn_down_b = w_ffn_up.astype(BF16), w_ffn_down.astype(BF16)
    for l in range(depth):
        shift1, scale1, gate1, shift2, scale2, gate2 = (m.reshape(bsz, 1, d) for m in jnp.split(mod[l], 6, axis=-1))

        main = _norm_matmul_conv(x2, row(norm1_w[l]), shift1, scale1, w_in_b, w_gates_b, l, ssm_conv_w[l].astype(F32),
                                 ssm_conv_b[l].astype(F32), seq=seq, tm=1024, tn=1024, n_out=MAIN_DIM,
                                 silu_cols=MAIN_XBC, conv_col0=MAIN_XBC, val_col0=None, name="in_proj_main")
        small = _norm_matmul(x2, row(norm1_w[l]), shift1, scale1, w_small_b, l, seq=seq, tm=1024, tn=SMALL_DIM,
                             out_dtype=F32, name="in_proj_small")
        y_ssd = _ssd(main, small, ssm_dt_bias[l], ssm_a_log[l], ssm_d[l], ssm_norm_w[l], bsz=bsz, seq=seq,
                     step_chunks=4)
        q, k, vt = _mla_prep(small, tab, mla_q_norm_w[l], mla_kv_norm_w[l], w_q_up[l], w_kv_up[l],
                             qk_norm_q_w[l], qk_norm_k_w[l], bsz=bsz, seq=seq, tm=512)
        o_mla = _attention(q, k, vt, bsz=bsz, seq=seq, tq=512)
        merged = _merge(y_ssd, o_mla, w_ssm_out_b, w_mla_out_b, l, main, tm=1024, tn=512)
        x2 = _matmul_residual(merged, w_mix_out_b, l, x2, gate1, seq=seq, tm=1024, tn=1024, name="mix_out")

        act = _norm_matmul_conv(x2, row(norm2_w[l]), shift2, scale2, w_ffn_up_b, w_ffn_up_b, l,
                                ffn_conv_w[l].astype(F32), ffn_conv_b[l].astype(F32), seq=seq, tm=1024, tn=512,
                                n_out=FFN_DIM, silu_cols=0, conv_col0=0, val_col0=FFN_DIM, name="ffn_up_glu")
        x2 = _matmul_residual(act, w_ffn_down_b, l, x2, gate2, seq=seq, tm=1024, tn=512, name="ffn_down")
    return x2.reshape(bsz, seq, d)
```

```python
import functools
import math

import jax
import jax.numpy as jnp
import numpy as np
from jax import lax
from jax.experimental import pallas as pl
from jax.experimental.pallas import tpu as pltpu

F32 = jnp.float32
BF16 = jnp.bfloat16

D_MODEL = 2048
CHUNK = 64
EPS = 1e-6

SSM_D_INNER = 2 * D_MODEL
SSM_HEAD_DIM = 64
SSM_HEADS = SSM_D_INNER // SSM_HEAD_DIM
SSM_GROUPS = 8
SSM_HEADS_PER_GROUP = SSM_HEADS // SSM_GROUPS
SSM_STATE = 128
SSM_CONV = 4
SSM_BC = SSM_GROUPS * SSM_STATE
SSM_CONV_DIM = SSM_D_INNER + 2 * SSM_BC
SSM_GROUP_WIDTH = SSM_HEADS_PER_GROUP * SSM_HEAD_DIM

MLA_HEADS = 16
MLA_Q_RANK = 512
MLA_KV_RANK = 512
MLA_NOPE = 128
MLA_ROPE = 64
MLA_V = 128
MLA_QK = MLA_NOPE + MLA_ROPE
MLA_OUT = MLA_HEADS * MLA_V
MLA_QK_PAD = 256
ROPE_BASE = 10000.0
ATTN_SCALE = MLA_QK ** -0.5

FFN_DIM = 5632
FFN_CONV = 3

IN_SIZES = (SSM_D_INNER, SSM_CONV_DIM, SSM_HEADS, MLA_Q_RANK, MLA_KV_RANK, MLA_ROPE, D_MODEL, D_MODEL)
IN_OFFS = tuple(int(v) for v in np.cumsum((0,) + IN_SIZES))

MAIN_Z = 0
MAIN_XBC = SSM_D_INNER
MAIN_GA = MAIN_XBC + SSM_CONV_DIM
MAIN_GB = MAIN_GA + D_MODEL
MAIN_DIM = MAIN_GB + D_MODEL
SMALL_QLAT = 0
SMALL_KVLAT = MLA_Q_RANK
SMALL_ROPE = SMALL_KVLAT + MLA_KV_RANK
SMALL_DT = SMALL_ROPE + 2 * MLA_ROPE
SMALL_DIM = SMALL_DT + 128

LANES = 128
HALO = 16
LOG2E = math.log2(math.e)
VMEM_LIMIT_MB = 56


def _call(kernel, *, name, grid, in_specs, out_specs, out_shape, semantics, scratch=()):
    return pl.pallas_call(
        kernel, grid=grid, in_specs=in_specs, out_specs=out_specs, out_shape=out_shape,
        scratch_shapes=list(scratch), name=name,
        compiler_params=pltpu.CompilerParams(dimension_semantics=semantics,
                                             vmem_limit_bytes=VMEM_LIMIT_MB * 1024 * 1024))


def _silu(v):
    return v * jax.nn.sigmoid(v)


def _rms(v, w):
    ms = jnp.mean(v * v, axis=-1, keepdims=True)
    return v * lax.rsqrt(ms + EPS) * w


def _mod_kernel(c_ref, w_ref, b_ref, o_ref):
    cond = _silu(c_ref[...]).astype(BF16)
    o_ref[0] = jnp.dot(cond, w_ref[0].astype(BF16), preferred_element_type=F32) + b_ref[0]


def _modulation(c, w_mod, b_mod):
    depth, d, n = w_mod.shape
    bsz = c.shape[0]
    tn = 1024
    c8 = jnp.zeros((8, d), F32).at[:bsz].set(c)
    out = _call(
        _mod_kernel, name="adaln_mod", grid=(depth, n // tn),
        in_specs=[pl.BlockSpec((8, d), lambda l, j: (0, 0)),
                  pl.BlockSpec((1, d, tn), lambda l, j: (l, 0, j)),
                  pl.BlockSpec((1, 1, tn), lambda l, j: (l, 0, j))],
        out_specs=pl.BlockSpec((1, 8, tn), lambda l, j: (l, 0, j)),
        out_shape=jax.ShapeDtypeStruct((depth, 8, n), F32),
        semantics=("parallel", "parallel"))(c8, w_mod, b_mod.reshape(depth, 1, n))
    return out[:, :bsz]


def _rope_kernel(pos_ref, freq_ref, o_ref):
    ang = pos_ref[...].astype(F32) * freq_ref[...]
    lane = lax.broadcasted_iota(jnp.int32, ang.shape, 1)
    cos = jnp.cos(ang)
    sin = jnp.sin(ang)
    o_ref[...] = jnp.where(lane < 64, cos, jnp.where(lane < 96, -sin, sin))


def _rope_table(positions):
    t = positions.size
    rows = 2048
    half = MLA_ROPE // 2
    inv_freq = (ROPE_BASE ** (-np.arange(0, MLA_ROPE, 2, dtype=np.float32) / MLA_ROPE)).astype(np.float32)
    freq = jnp.asarray(np.tile(inv_freq, LANES // half)[None, :])
    return _call(
        _rope_kernel, name="rope_table", grid=(t // rows,),
        in_specs=[pl.BlockSpec((rows, 1), lambda i: (i, 0)),
                  pl.BlockSpec((1, LANES), lambda i: (0, 0))],
        out_specs=pl.BlockSpec((rows, LANES), lambda i: (i, 0)),
        out_shape=jax.ShapeDtypeStruct((t, LANES), F32),
        semantics=("parallel",))(positions.reshape(t, 1), freq)


def _norm_mm_kernel(x_ref, nw_ref, sh_ref, sc_ref, w_ref, o_ref, h_ref, *, tm, rows):
    @pl.when(pl.program_id(1) == 0)
    def _():
        nw = nw_ref[...]
        sc = 1.0 + sc_ref[0]
        sh = sh_ref[0]

        def body(r, carry):
            sl = pl.ds(pl.multiple_of(r * rows, rows), rows)
            h_ref[sl, :] = (_rms(x_ref[sl, :], nw) * sc + sh).astype(BF16)
            return carry

        lax.fori_loop(0, tm // rows, body, 0)

    o_ref[...] = jnp.dot(h_ref[...], w_ref[0], preferred_element_type=F32).astype(o_ref.dtype)


def _norm_matmul(x2, nw, shift, scale, w, layer, *, seq, tm, tn, out_dtype, name):
    t, d = x2.shape
    n = w.shape[2]
    tpb = seq // tm
    kernel = functools.partial(_norm_mm_kernel, tm=tm, rows=256)
    return _call(
        kernel, name=name, grid=(t // tm, n // tn),
        in_specs=[pl.BlockSpec((tm, d), lambda i, j: (i, 0)),
                  pl.BlockSpec((1, d), lambda i, j: (0, 0)),
                  pl.BlockSpec((1, 1, d), lambda i, j: (i // tpb, 0, 0)),
                  pl.BlockSpec((1, 1, d), lambda i, j: (i // tpb, 0, 0)),
                  pl.BlockSpec((1, d, tn), lambda i, j: (layer, 0, j))],
        out_specs=pl.BlockSpec((tm, tn), lambda i, j: (i, j)),
        out_shape=jax.ShapeDtypeStruct((t, n), out_dtype),
        scratch=[pltpu.VMEM((tm, d), BF16)],
        semantics=("parallel", "arbitrary"))(x2, nw, shift, scale, w)


def _mm_resid_kernel(a_ref, w_ref, r_ref, g_ref, o_ref):
    acc = jnp.dot(a_ref[...], w_ref[0], preferred_element_type=F32)
    o_ref[...] = r_ref[...] + g_ref[0] * acc


def _matmul_residual(a, w, layer, res, gate, *, seq, tm, tn, name):
    t, k = a.shape
    n = w.shape[2]
    tpb = seq // tm
    return _call(
        _mm_resid_kernel, name=name, grid=(t // tm, n // tn),
        in_specs=[pl.BlockSpec((tm, k), lambda i, j: (i, 0)),
                  pl.BlockSpec((1, k, tn), lambda i, j: (layer, 0, j)),
                  pl.BlockSpec((tm, tn), lambda i, j: (i, j)),
                  pl.BlockSpec((1, 1, tn), lambda i, j: (i // tpb, 0, j))],
        out_specs=pl.BlockSpec((tm, tn), lambda i, j: (i, j)),
        out_shape=jax.ShapeDtypeStruct((t, n), F32),
        semantics=("parallel", "parallel"))(a, w, res, gate)


def _merge_kernel(y_ref, o_ref, ws_ref, wm_ref, ga_ref, gb_ref, out_ref):
    ys = jnp.dot(y_ref[...], ws_ref[0], preferred_element_type=F32)
    o = jnp.concatenate([o_ref[0, h] for h in range(o_ref.shape[1])], axis=1)
    ym = jnp.dot(o, wm_ref[0], preferred_element_type=F32)
    ga = jax.nn.sigmoid(ga_ref[...].astype(F32))
    gb = jax.nn.sigmoid(gb_ref[...].astype(F32))
    out_ref[...] = (ga * ys + gb * ym).astype(out_ref.dtype)


def _merge(y_ssd, o_mla, w_ssm_out, w_mla_out, layer, main, *, tm, tn):
    t = y_ssd.shape[0]
    n = w_ssm_out.shape[2]
    _, heads, seq, dv = o_mla.shape
    tpb = seq // tm
    ga_blk = MAIN_GA // tn
    gb_blk = MAIN_GB // tn
    return _call(
        _merge_kernel, name="branch_merge", grid=(t // tm, n // tn),
        in_specs=[pl.BlockSpec((tm, y_ssd.shape[1]), lambda i, j: (i, 0)),
                  pl.BlockSpec((1, heads, tm, dv), lambda i, j: (i // tpb, 0, i % tpb, 0)),
                  pl.BlockSpec((1, w_ssm_out.shape[1], tn), lambda i, j: (layer, 0, j)),
                  pl.BlockSpec((1, w_mla_out.shape[1], tn), lambda i, j: (layer, 0, j)),
                  pl.BlockSpec((tm, tn), lambda i, j: (i, ga_blk + j)),
                  pl.BlockSpec((tm, tn), lambda i, j: (i, gb_blk + j))],
        out_specs=pl.BlockSpec((tm, tn), lambda i, j: (i, j)),
        out_shape=jax.ShapeDtypeStruct((t, n), BF16),
        semantics=("parallel", "parallel"))(y_ssd, o_mla, w_ssm_out, w_mla_out, main, main)


def _norm_mm_conv_kernel(x_ref, xh_ref, nw_ref, sh_ref, sc_ref, w_ref, w2_ref, cw_ref, cb_ref, o_ref, h_ref, *,
                         tm, rows, taps, silu_hi, conv_lo, conv_hi, glu, tiles_per_seq):
    i = pl.program_id(0)
    j = pl.program_id(1)

    @pl.when(j == 0)
    def _():
        nw = nw_ref[...]
        sc = 1.0 + sc_ref[0]
        sh = sh_ref[0]
        h_ref[0:HALO, :] = (_rms(xh_ref[...], nw) * sc + sh).astype(BF16)

        def body(r, carry):
            src = pl.ds(pl.multiple_of(r * rows, rows), rows)
            dst = pl.ds(pl.multiple_of(HALO + r * rows, HALO), rows)
            h_ref[dst, :] = (_rms(x_ref[src, :], nw) * sc + sh).astype(BF16)
            return carry

        lax.fori_loop(0, tm // rows, body, 0)

    def plain(act, wref):
        r = jnp.dot(h_ref[HALO:, :], wref[0], preferred_element_type=F32)
        o_ref[...] = (_silu(r) if act else r).astype(o_ref.dtype)

    def conv():
        at_seq_start = (i % tiles_per_seq) == 0
        cw = cw_ref[...]
        cbias = cb_ref[...]
        g = jnp.dot(h_ref[...], w_ref[0], preferred_element_type=F32)
        g = jnp.concatenate([jnp.where(at_seq_start, 0.0, g[0:HALO, :]), g[HALO:, :]], axis=0)
        acc = cw[0:1, :] * g
        for k in range(1, taps):
            acc = pltpu.roll(acc, 1, axis=0) + cw[k:k + 1, :] * g
        y = _silu(acc[HALO:, :] + cbias)
        if glu:
            y = y * jnp.dot(h_ref[HALO:, :], w2_ref[0], preferred_element_type=F32)
        o_ref[...] = y.astype(o_ref.dtype)

    if glu:
        conv()
    else:
        pl.when(j < silu_hi)(functools.partial(plain, True, w_ref))
        pl.when(jnp.logical_and(j >= conv_lo, j < conv_hi))(conv)
        pl.when(j >= conv_hi)(functools.partial(plain, False, w2_ref))


def _norm_matmul_conv(x2, nw, shift, scale, w, w2, layer, conv_w, conv_b, *, seq, tm, tn, n_out, silu_cols, conv_col0,
                      val_col0, name):
    t, d = x2.shape
    taps, cwidth = conv_w.shape
    glu = val_col0 is not None
    tpb = seq // tm
    hb = tm // HALO
    conv_lo = conv_col0 // tn
    n_conv = cwidth // tn
    conv_hi = conv_lo + n_conv
    assert glu or silu_cols == conv_col0
    kernel = functools.partial(_norm_mm_conv_kernel, tm=tm, rows=256, taps=taps, silu_hi=silu_cols // tn,
                               conv_lo=conv_lo, conv_hi=conv_hi, glu=glu, tiles_per_seq=tpb)
    conv_idx = lambda i, j: (0, jnp.clip(j - conv_lo, 0, n_conv - 1))
    if glu:
        vb = val_col0 // tn
        w_idx = lambda i, j: (layer, 0, j)
        w2_idx = lambda i, j: (layer, 0, vb + j)
    else:
        w_idx = lambda i, j: (layer, 0, jnp.minimum(j, conv_hi - 1))
        w2_idx = lambda i, j: (layer, 0, jnp.maximum(j - conv_hi, 0))
    return _call(
        kernel, name=name, grid=(t // tm, n_out // tn),
        in_specs=[pl.BlockSpec((tm, d), lambda i, j: (i, 0)),
                  pl.BlockSpec((HALO, d), lambda i, j: (jnp.maximum(i * hb - 1, 0), 0)),
                  pl.BlockSpec((1, d), lambda i, j: (0, 0)),
                  pl.BlockSpec((1, 1, d), lambda i, j: (i // tpb, 0, 0)),
                  pl.BlockSpec((1, 1, d), lambda i, j: (i // tpb, 0, 0)),
                  pl.BlockSpec((1, d, tn), w_idx),
                  pl.BlockSpec((1, d, tn), w2_idx),
                  pl.BlockSpec((taps, tn), conv_idx),
                  pl.BlockSpec((1, tn), conv_idx)],
        out_specs=pl.BlockSpec((tm, tn), lambda i, j: (i, j)),
        out_shape=jax.ShapeDtypeStruct((t, n_out), BF16),
        scratch=[pltpu.VMEM((tm + HALO, d), BF16)],
        semantics=("parallel", "arbitrary"))(x2, x2, nw, shift, scale, w, w2, conv_w, conv_b.reshape(1, cwidth))


def _split3(v):
    hi = v.astype(BF16).astype(F32)
    r1 = v - hi
    mid = r1.astype(BF16).astype(F32)
    lo = (r1 - mid).astype(BF16).astype(F32)
    return hi, mid, lo


def _pack3(v, lane):
    hi, mid, lo = _split3(v)
    first = lane < CHUNK
    p1 = jnp.where(first, hi, pltpu.roll(mid, CHUNK, axis=1))
    p2 = jnp.where(first, lo, 0.0)
    return jnp.concatenate([p1, p2], axis=1).astype(BF16)


def _ssd_kernel(xa_ref, b_ref, c_ref, z_ref, dt_ref, dtb_ref, alog_ref, dskip_ref, nw_ref, tri_ref, rep_ref,
                o_ref, state, y_scr):
    q = CHUNK
    gw = SSM_GROUP_WIDTH

    @pl.when(pl.program_id(1) == 0)
    def _():
        state[...] = jnp.zeros_like(state)

    lane = lax.broadcasted_iota(jnp.int32, (q, LANES), 1)
    row = lax.broadcasted_iota(jnp.int32, (q, gw), 0)
    col = lax.broadcasted_iota(jnp.int32, (q, gw), 1) % q
    causal_bias = jnp.where(col <= row, 0.0, -jnp.inf)
    diag = col == row
    bd_r = lax.broadcasted_iota(jnp.int32, (4 * q, 4 * q), 0) // q
    bd_c = lax.broadcasted_iota(jnp.int32, (4 * q, 4 * q), 1) // q
    blockdiag = bd_r == bd_c

    for ci in range(o_ref.shape[0] // q):
        _ssd_chunk(slice(ci * q, (ci + 1) * q), y_scr.at[ci], lane, causal_bias, diag, blockdiag,
                   xa_ref, b_ref, c_ref, z_ref, dt_ref, dtb_ref, alog_ref, dskip_ref, nw_ref, tri_ref, rep_ref,
                   o_ref, state)


def _ssd_chunk(rs, y_scr, lane, causal_bias, diag, blockdiag,
               xa_ref, b_ref, c_ref, z_ref, dt_ref, dtb_ref, alog_ref, dskip_ref, nw_ref, tri_ref, rep_ref,
               o_ref, state):
    q = CHUNK
    gw = SSM_GROUP_WIDTH
    x_dt = dt_ref[rs, :] + dtb_ref[...]
    dt = jnp.maximum(x_dt, 0.0) + jnp.log1p(jnp.exp(-jnp.abs(x_dt)))
    dta = dt * (-jnp.exp(alog_ref[...])) * LOG2E

    hi, mid, lo = _split3(dta)
    stacked = jnp.concatenate([hi, mid, lo, jnp.zeros_like(hi)], axis=0).astype(BF16)
    acs = jnp.dot(tri_ref[...], stacked, preferred_element_type=F32)

    lhs = jnp.concatenate([_pack3(acs, lane), _pack3(dt, lane)], axis=0)

    def matmuls_on_inputs(g):
        gs = slice(g * gw, (g + 1) * gw)
        ns = slice(g * SSM_STATE, (g + 1) * SSM_STATE)
        rg = jnp.dot(lhs, rep_ref[:, gs], preferred_element_type=F32)
        bg = b_ref[rs, ns]
        cg = c_ref[rs, ns]
        b_tiled = jnp.concatenate([bg] * SSM_HEADS_PER_GROUP, axis=0)
        cb = lax.dot_general(cg, b_tiled, (((1,), (1,)), ((), ())), preferred_element_type=F32)
        st = state[g]
        y_off = jnp.dot(cg, st.astype(BF16), preferred_element_type=F32)
        return rg, bg, cb, st, y_off

    def finish_group(g, rg, bg, cb, st, y_off):
        gs = slice(g * gw, (g + 1) * gw)
        colb = rg[:q]
        dtrep = rg[q:]
        acs_row = jnp.sum(jnp.where(diag, colb, 0.0), axis=0, keepdims=True)
        m = (cb * jnp.exp2(colb - acs_row + causal_bias)).astype(BF16)
        xf = xa_ref[rs, gs].astype(F32)
        xdt = xf * dtrep
        xdt_b = xdt.astype(BF16)
        y_parts = []
        for quad in range(2):
            xs = xdt_b[:, quad * 4 * q:(quad + 1) * 4 * q]
            x_bd = jnp.where(blockdiag, jnp.concatenate([xs] * 4, axis=0), jnp.zeros((), BF16))
            y_parts.append(jnp.dot(m[:, quad * 4 * q:(quad + 1) * 4 * q], x_bd, preferred_element_type=F32))
        y_diag = jnp.concatenate(y_parts, axis=1)
        last = colb[q - 1:q, :]
        xd = (xdt * jnp.exp2(last - colb)).astype(BF16)
        state[g] = st * jnp.exp2(last) + lax.dot_general(bg, xd, (((0,), (0,)), ((), ())),
                                                         preferred_element_type=F32)
        y = y_diag + y_off * jnp.exp2(colb) + xf * dskip_ref[:, gs]
        gated = y * z_ref[rs, gs].astype(F32)
        y_scr[:, gs] = gated
        sq = gated * gated
        return sum(sq[:, k * LANES:(k + 1) * LANES] for k in range(gw // LANES))

    ahead = matmuls_on_inputs(0)
    sumsq = jnp.zeros((q, LANES), F32)
    for g in range(SSM_GROUPS):
        current = ahead
        if g + 1 < SSM_GROUPS:
            ahead = matmuls_on_inputs(g + 1)
        sumsq = sumsq + finish_group(g, *current)

    inv = lax.rsqrt(jnp.sum(sumsq, axis=-1, keepdims=True) / SSM_D_INNER + EPS)
    o_ref[rs, :] = (y_scr[...] * inv * nw_ref[...]).astype(o_ref.dtype)


def _ssd_constants():
    q = CHUNK
    tri = np.tril(np.ones((q, q), np.float32))
    tri4 = np.concatenate([tri, tri, tri, np.zeros_like(tri)], axis=1)
    rep = np.zeros((4 * q, SSM_D_INNER), np.float32)
    for j in range(3):
        for r in range(SSM_HEADS):
            rep[j * q + r, r * SSM_HEAD_DIM:(r + 1) * SSM_HEAD_DIM] = 1.0
    return jnp.asarray(tri4, BF16), jnp.asarray(rep, BF16)


def _ssd(main, small, dt_bias, a_log, d_skip, norm_w, *, bsz, seq, step_chunks):
    t = bsz * seq
    q = CHUNK
    rows = step_chunks * q
    ns = seq // rows
    tri4, rep = _ssd_constants()
    pad = lambda v: jnp.concatenate([v.astype(F32), jnp.zeros((LANES - SSM_HEADS,), F32)]).reshape(1, LANES)
    dskip_row = jnp.repeat(d_skip.astype(F32), SSM_HEAD_DIM).reshape(1, SSM_D_INNER)
    rowmap = lambda b, c: b * ns + c
    return _call(
        _ssd_kernel, name="ssd_scan", grid=(bsz, ns),
        in_specs=[pl.BlockSpec((rows, SSM_D_INNER), lambda b, c: (rowmap(b, c), MAIN_XBC // SSM_D_INNER)),
                  pl.BlockSpec((rows, SSM_BC), lambda b, c: (rowmap(b, c), (MAIN_XBC + SSM_D_INNER) // SSM_BC)),
                  pl.BlockSpec((rows, SSM_BC), lambda b, c: (rowmap(b, c), (MAIN_XBC + SSM_D_INNER) // SSM_BC + 1)),
                  pl.BlockSpec((rows, SSM_D_INNER), lambda b, c: (rowmap(b, c), MAIN_Z // SSM_D_INNER)),
                  pl.BlockSpec((rows, LANES), lambda b, c: (rowmap(b, c), SMALL_DT // LANES)),
                  pl.BlockSpec((1, LANES), lambda b, c: (0, 0)),
                  pl.BlockSpec((1, LANES), lambda b, c: (0, 0)),
                  pl.BlockSpec((1, SSM_D_INNER), lambda b, c: (0, 0)),
                  pl.BlockSpec((1, SSM_D_INNER), lambda b, c: (0, 0)),
                  pl.BlockSpec((q, 4 * q), lambda b, c: (0, 0)),
                  pl.BlockSpec((4 * q, SSM_D_INNER), lambda b, c: (0, 0))],
        out_specs=pl.BlockSpec((rows, SSM_D_INNER), lambda b, c: (rowmap(b, c), 0)),
        out_shape=jax.ShapeDtypeStruct((t, SSM_D_INNER), BF16),
        scratch=[pltpu.VMEM((SSM_GROUPS, SSM_STATE, SSM_GROUP_WIDTH), F32),
                 pltpu.VMEM((step_chunks, q, SSM_D_INNER), F32)],
        semantics=("parallel", "arbitrary"))(
            main, main, main, main, small, pad(dt_bias), pad(a_log), dskip_row,
            norm_w.astype(F32).reshape(1, SSM_D_INNER), tri4, rep)


def _mla_prep_kernel(ql_ref, kvl_ref, rope_ref, tab_ref, qnw_ref, kvnw_ref, wq_ref, wk_ref, wvt_ref, gq_ref, gk_ref,
                     q_ref, k_ref, vt_ref):
    qn = _rms(ql_ref[...], qnw_ref[...]).astype(BF16)
    kvn = _rms(kvl_ref[...], kvnw_ref[...]).astype(BF16)
    q = jnp.dot(qn, wq_ref[...], preferred_element_type=F32)
    kn = jnp.dot(kvn, wk_ref[...], preferred_element_type=F32)
    vt = lax.dot_general(wvt_ref[...], kvn, (((1,), (1,)), ((), ())), preferred_element_type=F32).astype(BF16)
    for h in range(MLA_HEADS):
        vt_ref[0, h, 0] = vt[h * MLA_V:(h + 1) * MLA_V, :]
    tab = tab_ref[...]
    rope = rope_ref[...]
    lane = lax.broadcasted_iota(jnp.int32, rope.shape, 1)
    first = lane < MLA_ROPE
    gq = gq_ref[...]
    gk = gk_ref[...]
    ss_rope = jnp.sum(jnp.where(first, rope * rope, 0.0), axis=-1, keepdims=True)
    kr = rope * (gk[:, MLA_NOPE:] * tab)
    kr = kr + pltpu.roll(kr, MLA_ROPE, axis=1)
    q_rope_mul = gq[:, MLA_NOPE:] * tab * ATTN_SCALE
    q_nope_mul = gq[:, :MLA_NOPE] * ATTN_SCALE
    for h in range(MLA_HEADS):
        c0 = h * MLA_QK_PAD
        qa = q[:, c0:c0 + MLA_NOPE]
        qb = q[:, c0 + MLA_NOPE:c0 + MLA_QK_PAD]
        ss = (jnp.sum(qa * qa, axis=-1, keepdims=True)
              + jnp.sum(jnp.where(first, qb * qb, 0.0), axis=-1, keepdims=True))
        inv = lax.rsqrt(ss / MLA_QK + EPS)
        q_ref[0, h, :, :MLA_NOPE] = (qa * inv * q_nope_mul).astype(BF16)
        q_ref[0, h, :, MLA_NOPE:] = (qb * inv * q_rope_mul).astype(BF16)
        ka = kn[:, h * MLA_NOPE:(h + 1) * MLA_NOPE]
        ssk = jnp.sum(ka * ka, axis=-1, keepdims=True) + ss_rope
        invk = lax.rsqrt(ssk / MLA_QK + EPS)
        k_ref[0, h, :, :MLA_NOPE] = (ka * invk * gk[:, :MLA_NOPE]).astype(BF16)
        k_ref[0, h, :, MLA_NOPE:] = (kr * invk).astype(BF16)


def _swap_halves(v):
    half = v.shape[-1] // 2
    return jnp.concatenate([v[..., half:], v[..., :half]], axis=-1)


def _mla_prep(small, tab, q_norm_w, kv_norm_w, w_q_up, w_kv_up, qn_w, kn_w, *, bsz, seq, tm):
    t = small.shape[0]
    tpb = seq // tm
    wq = w_q_up.reshape(MLA_Q_RANK, MLA_HEADS, MLA_QK)
    wq = jnp.concatenate([wq, _swap_halves(wq[..., MLA_NOPE:])], axis=-1)
    wq = wq.reshape(MLA_Q_RANK, MLA_HEADS * MLA_QK_PAD).astype(BF16)
    wkv = w_kv_up.reshape(MLA_KV_RANK, MLA_HEADS, MLA_NOPE + MLA_V)
    wk = wkv[..., :MLA_NOPE].reshape(MLA_KV_RANK, MLA_HEADS * MLA_NOPE).astype(BF16)
    wvt = wkv[..., MLA_NOPE:].reshape(MLA_KV_RANK, MLA_OUT).T.astype(BF16)
    row = lambda g: jnp.concatenate([g, _swap_halves(g[MLA_NOPE:])]).astype(F32).reshape(1, MLA_QK_PAD)
    n = MLA_HEADS * MLA_QK_PAD
    return _call(
        _mla_prep_kernel, name="mla_prep", grid=(t // tm,),
        in_specs=[pl.BlockSpec((tm, MLA_Q_RANK), lambda i: (i, SMALL_QLAT // MLA_Q_RANK)),
                  pl.BlockSpec((tm, MLA_KV_RANK), lambda i: (i, SMALL_KVLAT // MLA_KV_RANK)),
                  pl.BlockSpec((tm, LANES), lambda i: (i, SMALL_ROPE // LANES)),
                  pl.BlockSpec((tm, LANES), lambda i: (i, 0)),
                  pl.BlockSpec((1, MLA_Q_RANK), lambda i: (0, 0)),
                  pl.BlockSpec((1, MLA_KV_RANK), lambda i: (0, 0)),
                  pl.BlockSpec((MLA_Q_RANK, n), lambda i: (0, 0)),
                  pl.BlockSpec((MLA_KV_RANK, MLA_HEADS * MLA_NOPE), lambda i: (0, 0)),
                  pl.BlockSpec((MLA_OUT, MLA_KV_RANK), lambda i: (0, 0)),
                  pl.BlockSpec((1, MLA_QK_PAD), lambda i: (0, 0)),
                  pl.BlockSpec((1, MLA_QK_PAD), lambda i: (0, 0))],
        out_specs=[pl.BlockSpec((1, MLA_HEADS, tm, MLA_QK_PAD), lambda i: (i // tpb, 0, i % tpb, 0)),
                   pl.BlockSpec((1, MLA_HEADS, tm, MLA_QK_PAD), lambda i: (i // tpb, 0, i % tpb, 0)),
                   pl.BlockSpec((1, MLA_HEADS, 1, MLA_V, tm), lambda i: (i // tpb, 0, i % tpb, 0, 0))],
        out_shape=[jax.ShapeDtypeStruct((bsz, MLA_HEADS, seq, MLA_QK_PAD), BF16),
                   jax.ShapeDtypeStruct((bsz, MLA_HEADS, seq, MLA_QK_PAD), BF16),
                   jax.ShapeDtypeStruct((bsz, MLA_HEADS, tpb, MLA_V, tm), BF16)],
        semantics=("parallel",))(
            small, small, small, tab, q_norm_w.astype(F32).reshape(1, MLA_Q_RANK),
            kv_norm_w.astype(F32).reshape(1, MLA_KV_RANK), wq, wk, wvt, row(qn_w), row(kn_w))


def _attn_kernel(q_ref, k_ref, vt_ref, o_ref, s_buf, p_buf, *, tq, nq):
    n_tiles = nq * (nq + 1) // 2
    nt_dims = (((1,), (1,)), ((), ()))

    def rows(idx):
        return pl.ds(pl.multiple_of(idx * tq, tq), tq)

    def finish(qi_p, kj_p, alpha_p, l_p, acc, slot_p):
        acc = alpha_p * acc + jnp.dot(vt_ref[0, 0, kj_p], p_buf[slot_p], preferred_element_type=F32)
        o_ref[0, 0, rows(qi_p), :] = (acc / l_p).T.astype(o_ref.dtype)
        return acc

    def step(t, carry, *, masked):
        qi, kj, qi_p, kj_p, m, l, alpha_p, l_p, acc = carry
        slot = t % 2
        acc = finish(qi_p, kj_p, alpha_p, l_p, acc, 1 - slot)

        s = s_buf[slot]
        if masked:
            kc = lax.broadcasted_iota(jnp.int32, s.shape, 0) // CHUNK
            qc = lax.broadcasted_iota(jnp.int32, s.shape, 1) // CHUNK
            s = jnp.where(kc <= qc, s, -jnp.inf)
        m = jnp.where(kj == 0, -jnp.inf, m)
        m_new = jnp.maximum(m, jnp.max(s, axis=0, keepdims=True))
        p = jnp.exp(s - m_new)
        alpha = jnp.exp(m - m_new)
        l = alpha * l + jnp.sum(p, axis=0, keepdims=True)
        p_buf[slot] = p.astype(BF16)

        last = kj == qi
        qi_n = jnp.minimum(jnp.where(last, qi + 1, qi), nq - 1)
        kj_n = jnp.where(last, 0, kj + 1)
        s_buf[1 - slot] = lax.dot_general(k_ref[0, 0, rows(kj_n), :], q_ref[0, 0, rows(qi_n), :], nt_dims,
                                          preferred_element_type=F32)
        return qi_n, kj_n, qi, kj, m_new, l, alpha, l, acc

    def body(t, carry):
        return lax.cond(carry[0] == carry[1], functools.partial(step, masked=True),
                        functools.partial(step, masked=False), t, carry)

    s_buf[0] = lax.dot_general(k_ref[0, 0, rows(0), :], q_ref[0, 0, rows(0), :], nt_dims,
                               preferred_element_type=F32)
    p_buf[1] = jnp.zeros(p_buf.shape[1:], BF16)
    zero = jnp.zeros((), jnp.int32)
    init = (zero, zero, zero, zero, jnp.full((1, tq), -jnp.inf, F32), jnp.zeros((1, tq), F32),
            jnp.zeros((1, tq), F32), jnp.ones((1, tq), F32), jnp.zeros((MLA_V, tq), F32))
    _, _, qi_p, kj_p, _, _, alpha_p, l_p, acc = lax.fori_loop(0, n_tiles, body, init)
    finish(qi_p, kj_p, alpha_p, l_p, acc, (n_tiles - 1) % 2)


def _attention(q, k, vt, *, bsz, seq, tq):
    nq = seq // tq
    kernel = functools.partial(_attn_kernel, tq=tq, nq=nq)
    return _call(
        kernel, name="mla_attention", grid=(bsz, MLA_HEADS),
        in_specs=[pl.BlockSpec((1, 1, seq, MLA_QK_PAD), lambda b, h: (b, h, 0, 0)),
                  pl.BlockSpec((1, 1, seq, MLA_QK_PAD), lambda b, h: (b, h, 0, 0)),
                  pl.BlockSpec((1, 1, nq, MLA_V, tq), lambda b, h: (b, h, 0, 0, 0))],
        out_specs=pl.BlockSpec((1, 1, seq, MLA_V), lambda b, h: (b, h, 0, 0)),
        out_shape=jax.ShapeDtypeStruct((bsz, MLA_HEADS, seq, MLA_V), BF16),
        scratch=[pltpu.VMEM((2, tq, tq), F32), pltpu.VMEM((2, tq, tq), BF16)],
        semantics=("parallel", "parallel"))(q, k, vt)


def _in_proj_small_weights(w_in_b):
    seg = lambda i: w_in_b[:, :, IN_OFFS[i]:IN_OFFS[i + 1]]
    dt, q_lat, kv_lat, k_rope = seg(2), seg(3), seg(4), seg(5)
    pad = jnp.zeros(w_in_b.shape[:2] + (SMALL_DIM - SMALL_DT - SSM_HEADS,), w_in_b.dtype)
    return jnp.concatenate([q_lat, kv_lat, k_rope, _swap_halves(k_rope), dt, pad], axis=2)


def kernel(x, c, positions, norm1_w, norm2_w, w_mod, b_mod, w_in, ssm_conv_w, ssm_conv_b, ssm_dt_bias, ssm_a_log, ssm_d, ssm_norm_w, w_ssm_out, mla_q_norm_w, w_q_up, mla_kv_norm_w, w_kv_up, qk_norm_q_w, qk_norm_k_w, w_mla_out, w_mix_out, w_ffn_up, ffn_conv_w, ffn_conv_b, w_ffn_down):
    bsz, seq, d = x.shape
    depth = w_mod.shape[0]
    t = bsz * seq
    mod = _modulation(c, w_mod, b_mod)
    tab = _rope_table(positions)
    x2 = x.reshape(t, d)
    row = lambda v: v.astype(F32).reshape(1, -1)
    w_in_b = w_in.astype(BF16)
    w_gates_b = w_in_b[:, :, IN_OFFS[6]:]
    w_small_b = _in_proj_small_weights(w_in_b)
    w_ssm_out_b, w_mla_out_b, w_mix_out_b = w_ssm_out.astype(BF16), w_mla_out.astype(BF16), w_mix_out.astype(BF16)
    w_ffn_up_b, w_ffn_down_b = w_ffn_up.astype(BF16), w_ffn_down.astype(BF16)
    for l in range(depth):
        shift1, scale1, gate1, shift2, scale2, gate2 = (m.reshape(bsz, 1, d) for m in jnp.split(mod[l], 6, axis=-1))

        main = _norm_matmul_conv(x2, row(norm1_w[l]), shift1, scale1, w_in_b, w_gates_b, l, ssm_conv_w[l].astype(F32),
                                 ssm_conv_b[l].astype(F32), seq=seq, tm=1024, tn=1024, n_out=MAIN_DIM,
                                 silu_cols=MAIN_XBC, conv_col0=MAIN_XBC, val_col0=None, name="in_proj_main")
        small = _norm_matmul(x2, row(norm1_w[l]), shift1, scale1, w_small_b, l, seq=seq, tm=1024, tn=SMALL_DIM,
                             out_dtype=F32, name="in_proj_small")
        y_ssd = _ssd(main, small, ssm_dt_bias[l], ssm_a_log[l], ssm_d[l], ssm_norm_w[l], bsz=bsz, seq=seq,
                     step_chunks=4)
        q, k, vt = _mla_prep(small, tab, mla_q_norm_w[l], mla_kv_norm_w[l], w_q_up[l], w_kv_up[l],
                             qk_norm_q_w[l], qk_norm_k_w[l], bsz=bsz, seq=seq, tm=512)
        o_mla = _attention(q, k, vt, bsz=bsz, seq=seq, tq=512)
        merged = _merge(y_ssd, o_mla, w_ssm_out_b, w_mla_out_b, l, main, tm=1024, tn=512)
        x2 = _matmul_residual(merged, w_mix_out_b, l, x2, gate1, seq=seq, tm=1024, tn=1024, name="mix_out")

        act = _norm_matmul_conv(x2, row(norm2_w[l]), shift2, scale2, w_ffn_up_b, w_ffn_up_b, l,
                                ffn_conv_w[l].astype(F32), ffn_conv_b[l].astype(F32), seq=seq, tm=1024, tn=512,
                                n_out=FFN_DIM, silu_cols=0, conv_col0=0, val_col0=FFN_DIM, name="ffn_up_glu")
        x2 = _matmul_residual(act, w_ffn_down_b, l, x2, gate2, seq=seq, tm=1024, tn=512, name="ffn_down")
    return x2.reshape(bsz, seq, d)
```
